```python
import jax, jax.numpy as jnp
from jax import lax
import numpy as np

D_MODEL = 1024
BATCH = 8
SEQ = 4096
DEPTH = 2

HEAD_DIM = 64
ROT_DIM = HEAD_DIM // 4
ROPE_THETA = 500000.0
NORM_EPS = 1e-6
NEG_INF = -1e30

MOBA_HEADS = 8
MOBA_BLOCK = 256
MOBA_TOPK = 3
MOBA_Q_CHUNK = 32

NSA_HEADS = 8
NSA_KV_HEADS = 2
NSA_GROUP = NSA_HEADS // NSA_KV_HEADS
NSA_CMP_LEN = 32
NSA_CMP_STRIDE = 16
NSA_CMP_HIDDEN = 256
NSA_SEL_BLOCK = 64
NSA_SEL_TOPK = 16
NSA_WINDOW = 512
NSA_Q_CHUNK = 32
NSA_FORCE = 1e4

DIFF_HEADS = 8
DIFF_QK_DIM = HEAD_DIM
DIFF_V_DIM = 2 * HEAD_DIM
DIFF_Q_CHUNK = 128

FFN_HIDDEN = -(-8 * D_MODEL // (3 * 256)) * 256

MOBA_W = MOBA_HEADS * HEAD_DIM
NSA_Q_W = NSA_HEADS * HEAD_DIM
NSA_KV_W = 3 * 2 * NSA_KV_HEADS * HEAD_DIM
NSA_GATE_W = 3 * NSA_HEADS
SPARSE_IN = 3 * MOBA_W + NSA_Q_W + NSA_KV_W + NSA_GATE_W
SPARSE_SPLITS = [MOBA_W, 2 * MOBA_W, 3 * MOBA_W, 3 * MOBA_W + NSA_Q_W, 3 * MOBA_W + NSA_Q_W + NSA_KV_W]
SPARSE_OUT = MOBA_W + NSA_Q_W
DIFF_IN = 2 * DIFF_HEADS * 2 * DIFF_QK_DIM + DIFF_HEADS * DIFF_V_DIM
DIFF_OUT = DIFF_HEADS * DIFF_V_DIM

kernel_name = 'hybrid_moba_nsa_diffattn_adaln_block'


def rms_norm(t, gain):
    tf = t.astype(jnp.float32)
    y = tf * lax.rsqrt(jnp.mean(tf * tf, axis=-1, keepdims=True) + NORM_EPS)
    return (y * gain.astype(jnp.float32)).astype(t.dtype)


def rope_tables(positions):
    inv_freq = 1.0 / (ROPE_THETA ** (jnp.arange(0, ROT_DIM, 2, dtype=jnp.float32) / ROT_DIM))
    ang = positions.astype(jnp.float32)[:, None, :, None] * inv_freq
    return jnp.cos(ang), jnp.sin(ang)


def partial_rope(t, cos, sin):
    half = ROT_DIM // 2
    cos = cos.astype(t.dtype)
    sin = sin.astype(t.dtype)
    t1, t2, rest = t[..., :half], t[..., half:ROT_DIM], t[..., ROT_DIM:]
    return jnp.concatenate([t1 * cos - t2 * sin, t2 * cos + t1 * sin, rest], axis=-1)


def masked_softmax(s, valid):
    s = jnp.where(valid, s.astype(jnp.float32), NEG_INF)
    return jax.nn.softmax(s, axis=-1) * valid


def moba_attention(q, k, v):
    B, H, S, dh = q.shape
    Qc = MOBA_Q_CHUNK
    nb = -(-S // MOBA_BLOCK)
    pad = nb * MOBA_BLOCK - S
    kb = jnp.pad(k, ((0, 0), (0, 0), (0, pad), (0, 0))).reshape(B, H, nb, MOBA_BLOCK, dh)
    vb = jnp.pad(v, ((0, 0), (0, 0), (0, pad), (0, 0))).reshape(B, H, nb, MOBA_BLOCK, dh)
    kmean = jnp.mean(kb.astype(jnp.float32), axis=3)
    topk = min(MOBA_TOPK, nb)
    scale = dh ** -0.5
    bi = jnp.arange(B)[:, None, None, None]
    hi = jnp.arange(H)[None, :, None, None]
    blk = jnp.arange(nb)
    off = jnp.arange(MOBA_BLOCK)

    def one_chunk(ci):
        t0 = ci * Qc
        pos = t0 + jnp.arange(Qc)
        own = t0 // MOBA_BLOCK
        qc = lax.dynamic_slice_in_dim(q, t0, Qc, axis=2)
        gate = jnp.einsum('bhqd,bhnd->bhqn', qc.astype(jnp.float32), kmean)
        gate = jnp.where(blk < own, gate, NEG_INF)
        gval, idx = lax.top_k(gate, topk)
        sel_ok = gval > 0.5 * NEG_INF
        k_sel = kb[bi, hi, idx]
        v_sel = vb[bi, hi, idx]
        k_own = lax.dynamic_index_in_dim(kb, own, axis=2, keepdims=False)
        v_own = lax.dynamic_index_in_dim(vb, own, axis=2, keepdims=False)
        s_sel = jnp.einsum('bhqd,bhqnjd->bhqnj', qc, k_sel).reshape(B, H, Qc, topk * MOBA_BLOCK)
        s_own = jnp.einsum('bhqd,bhjd->bhqj', qc, k_own)
        ok_sel = jnp.broadcast_to(sel_ok[..., None], (B, H, Qc, topk, MOBA_BLOCK)).reshape(B, H, Qc, topk * MOBA_BLOCK)
        ok_own = jnp.broadcast_to((own * MOBA_BLOCK + off)[None, :] <= pos[:, None], (B, H, Qc, MOBA_BLOCK))
        p = masked_softmax(jnp.concatenate([s_sel, s_own], axis=-1) * scale,
                           jnp.concatenate([ok_sel, ok_own], axis=-1)).astype(v.dtype)
        p_sel = p[..., :topk * MOBA_BLOCK].reshape(B, H, Qc, topk, MOBA_BLOCK)
        p_own = p[..., topk * MOBA_BLOCK:]
        return (jnp.einsum('bhqnj,bhqnjd->bhqd', p_sel, v_sel)
                + jnp.einsum('bhqj,bhjd->bhqd', p_own, v_own))

    out = lax.map(one_chunk, jnp.arange(S // Qc))
    return out.transpose(1, 2, 0, 3, 4).reshape(B, H, S, dh)


def nsa_compress(t, pos_emb, w1, w2):
    S = t.shape[2]
    ncmp = (S - NSA_CMP_LEN) // NSA_CMP_STRIDE + 1
    idx = np.arange(ncmp)[:, None] * NSA_CMP_STRIDE + np.arange(NSA_CMP_LEN)[None, :]
    blocks = t[:, :, idx] + pos_emb
    hid = jax.nn.silu(jnp.einsum('bgnld,lde->bgne', blocks, w1))
    return jnp.einsum('bgne,ed->bgnd', hid, w2)


def nsa_overlap(ncmp, nsel):
    cs = np.arange(ncmp) * NSA_CMP_STRIDE
    ss = np.arange(nsel) * NSA_SEL_BLOCK
    ov = np.minimum(cs[:, None] + NSA_CMP_LEN, ss[None, :] + NSA_SEL_BLOCK) - np.maximum(cs[:, None], ss[None, :])
    return jnp.asarray(np.clip(ov, 0, None) / NSA_CMP_LEN, dtype=jnp.float32)


def nsa_attention(q, kc, vc, ks, vs, kw, vw, gates):
    B, H, S, dh = q.shape
    G, R, Qc, SB, W = NSA_KV_HEADS, NSA_GROUP, NSA_Q_CHUNK, NSA_SEL_BLOCK, NSA_WINDOW
    ncmp = kc.shape[2]
    nsel = S // SB
    topk = min(NSA_SEL_TOPK, nsel)
    scale = dh ** -0.5
    cmp_end = jnp.arange(ncmp) * NSA_CMP_STRIDE + NSA_CMP_LEN - 1
    overlap = nsa_overlap(ncmp, nsel)
    ksb = ks.reshape(B, G, nsel, SB, dh)
    vsb = vs.reshape(B, G, nsel, SB, dh)
    kwp = jnp.pad(kw, ((0, 0), (0, 0), (W, 0), (0, 0)))
    vwp = jnp.pad(vw, ((0, 0), (0, 0), (W, 0), (0, 0)))
    bi = jnp.arange(B)[:, None, None, None]
    gi = jnp.arange(G)[None, :, None, None]
    sel_blk = jnp.arange(nsel)
    sel_off = jnp.arange(SB)
    win_off = jnp.arange(W + Qc)

    def one_chunk(ci):
        t0 = ci * Qc
        pos = t0 + jnp.arange(Qc)
        qg = lax.dynamic_slice_in_dim(q, t0, Qc, axis=2).reshape(B, G, R, Qc, dh)
        g = lax.dynamic_slice_in_dim(gates, t0, Qc, axis=2).reshape(B, G, R, Qc, 3)
        ok_c = cmp_end[None, :] <= pos[:, None]
        p_c = masked_softmax(jnp.einsum('bgrqd,bgnd->bgrqn', qg, kc) * scale, ok_c)
        o_c = jnp.einsum('bgrqn,bgnd->bgrqd', p_c.astype(vc.dtype), vc)
        imp = jnp.einsum('bgqn,nm->bgqm', p_c.sum(axis=2), overlap)
        cur = pos // SB
        ok_blk = sel_blk[None, :] <= cur[:, None]
        forced = ok_blk & ((sel_blk[None, :] == 0) | (sel_blk[None, :] >= cur[:, None] - 1))
        score = jnp.where(ok_blk, jnp.where(forced, NSA_FORCE, imp), NEG_INF)
        sval, idx = lax.top_k(score, topk)
        sel_ok = sval > 0.5 * NEG_INF
        k_sel = ksb[bi, gi, idx]
        v_sel = vsb[bi, gi, idx]
        tok = idx[..., None] * SB + sel_off
        ok_s = (sel_ok[..., None] & (tok <= pos[None, None, :, None, None])).reshape(B, G, 1, Qc, topk * SB)
        s_s = jnp.einsum('bgrqd,bgqnjd->bgrqnj', qg, k_sel).reshape(B, G, R, Qc, topk * SB) * scale
        p_s = masked_softmax(s_s, ok_s).astype(vs.dtype).reshape(B, G, R, Qc, topk, SB)
        o_s = jnp.einsum('bgrqnj,bgqnjd->bgrqd', p_s, v_sel)
        k_w = lax.dynamic_slice_in_dim(kwp, t0, W + Qc, axis=2)
        v_w = lax.dynamic_slice_in_dim(vwp, t0, W + Qc, axis=2)
        kpos = t0 - W + win_off
        ok_w = (kpos[None, :] <= pos[:, None]) & (kpos[None, :] > pos[:, None] - W) & (kpos[None, :] >= 0)
        p_w = masked_softmax(jnp.einsum('bgrqd,bgkd->bgrqk', qg, k_w) * scale, ok_w)
        o_w = jnp.einsum('bgrqk,bgkd->bgrqd', p_w.astype(vw.dtype), v_w)
        o = g[..., 0:1] * o_c + g[..., 1:2] * o_s + g[..., 2:3] * o_w
        return o.reshape(B, H, Qc, dh)

    out = lax.map(one_chunk, jnp.arange(S // Qc))
    return out.transpose(1, 2, 0, 3, 4).reshape(B, H, S, dh)


def to_heads(t, n_heads):
    B, S, _ = t.shape
    return t.reshape(B, S, n_heads, HEAD_DIM).transpose(0, 2, 1, 3)


def qk_prep(t, gain, cos, sin):
    return partial_rope(rms_norm(t, gain), cos, sin)


def sparse_mixer(h, cos, sin, w_in, w_out, moba_qn, moba_kn, nsa_qn, nsa_kn, cmp_pos, cmp_w1, cmp_w2):
    B, S, _ = h.shape
    mq, mk, mv, nq, nkv, ng = jnp.split(h @ w_in, SPARSE_SPLITS, axis=-1)
    o_moba = moba_attention(qk_prep(to_heads(mq, MOBA_HEADS), moba_qn, cos, sin),
                            qk_prep(to_heads(mk, MOBA_HEADS), moba_kn, cos, sin),
                            to_heads(mv, MOBA_HEADS))
    kv = nkv.reshape(B, S, 3, 2, NSA_KV_HEADS, HEAD_DIM).transpose(2, 3, 0, 4, 1, 5)
    kc = nsa_compress(qk_prep(kv[0, 0], nsa_kn[0], cos, sin), cmp_pos[0], cmp_w1[0], cmp_w2[0])
    vc = nsa_compress(kv[0, 1], cmp_pos[1], cmp_w1[1], cmp_w2[1])
    gates = jax.nn.sigmoid(ng).reshape(B, S, 3, NSA_HEADS).transpose(0, 3, 1, 2)
    o_nsa = nsa_attention(qk_prep(to_heads(nq, NSA_HEADS), nsa_qn, cos, sin), kc, vc,
                          qk_prep(kv[1, 0], nsa_kn[1], cos, sin), kv[1, 1],
                          qk_prep(kv[2, 0], nsa_kn[2], cos, sin), kv[2, 1], gates)
    o = jnp.concatenate([o_moba, o_nsa], axis=1).transpose(0, 2, 1, 3).reshape(B, S, SPARSE_OUT)
    return o @ w_out


def diff_mixer(h, cos, sin, w_in, w_out, q_norm, k_norm, lam_params, out_norm, lam_init):
    B, S, _ = h.shape
    H, Qc = DIFF_HEADS, DIFF_Q_CHUNK
    qk_w = H * 2 * DIFF_QK_DIM
    q, k, v = jnp.split(h @ w_in, [qk_w, 2 * qk_w], axis=-1)
    cos5, sin5 = cos[:, :, None], sin[:, :, None]
    q = partial_rope(rms_norm(q.reshape(B, S, H, 2, DIFF_QK_DIM).transpose(0, 2, 3, 1, 4), q_norm), cos5, sin5)
    k = partial_rope(rms_norm(k.reshape(B, S, H, 2, DIFF_QK_DIM).transpose(0, 2, 3, 1, 4), k_norm), cos5, sin5)
    v = v.reshape(B, S, H, DIFF_V_DIM).transpose(0, 2, 1, 3)
    lp = lam_params.astype(jnp.float32)
    lam = jnp.exp(jnp.sum(lp[0] * lp[1])) - jnp.exp(jnp.sum(lp[2] * lp[3])) + lam_init
    scale = DIFF_QK_DIM ** -0.5
    key_idx = jnp.arange(S)

    def one_chunk(ci):
        t0 = ci * Qc
        pos = t0 + jnp.arange(Qc)
        qc = lax.dynamic_slice_in_dim(q, t0, Qc, axis=3)
        s = jnp.einsum('bhcqd,bhckd->bhcqk', qc, k) * scale
        p = masked_softmax(s, key_idx[None, :] <= pos[:, None])
        a = p[:, :, 0] - lam * p[:, :, 1]
        return jnp.einsum('bhqk,bhkd->bhqd', a.astype(v.dtype), v)

    o = lax.map(one_chunk, jnp.arange(S // Qc)).transpose(1, 2, 0, 3, 4).reshape(B, H, S, DIFF_V_DIM)
    o = rms_norm(o, out_norm) * (1.0 - lam_init)
    return o.transpose(0, 2, 1, 3).reshape(B, S, DIFF_OUT) @ w_out


def swiglu(h, w_gate, w_up, w_down):
    return (jax.nn.silu(h @ w_gate) * (h @ w_up)) @ w_down


def setup_inputs(seed: int = 0) -> dict:
    key = jax.random.key(seed)
    keys = iter(jax.random.split(key, 40))
    n_even = (DEPTH + 1) // 2
    n_odd = DEPTH // 2

    def nrm(shape, scale):
        return jax.random.normal(next(keys), shape, jnp.float32) * scale

    def gain(shape):
        return 1.0 + nrm(shape, 0.05)

    offsets = jax.random.randint(next(keys), (BATCH, 1), 0, 1024, dtype=jnp.int32)
    positions = offsets + jnp.arange(SEQ, dtype=jnp.int32)[None, :]
    return {
        'x': nrm((BATCH, SEQ, D_MODEL), 1.0),
        'c': nrm((BATCH, D_MODEL), 1.0),
        'positions': positions,
        'ada_w': nrm((DEPTH, D_MODEL, 6 * D_MODEL), 0.5 * D_MODEL ** -0.5),
        'ada_b': nrm((DEPTH, 6 * D_MODEL), 0.02),
        'attn_norm': gain((DEPTH, D_MODEL)),
        'ffn_norm': gain((DEPTH, D_MODEL)),
        'ffn_w_gate': nrm((DEPTH, D_MODEL, FFN_HIDDEN), D_MODEL ** -0.5),
        'ffn_w_up': nrm((DEPTH, D_MODEL, FFN_HIDDEN), D_MODEL ** -0.5),
        'ffn_w_down': nrm((DEPTH, FFN_HIDDEN, D_MODEL), FFN_HIDDEN ** -0.5),
        'sp_w_in': nrm((n_even, D_MODEL, SPARSE_IN), D_MODEL ** -0.5),
        'sp_w_out': nrm((n_even, SPARSE_OUT, D_MODEL), SPARSE_OUT ** -0.5),
        'moba_q_norm': gain((n_even, HEAD_DIM)),
        'moba_k_norm': gain((n_even, HEAD_DIM)),
        'nsa_q_norm': gain((n_even, HEAD_DIM)),
        'nsa_k_norm': gain((n_even, 3, HEAD_DIM)),
        'nsa_cmp_pos': nrm((n_even, 2, NSA_CMP_LEN, HEAD_DIM), 0.1),
        'nsa_cmp_w1': nrm((n_even, 2, NSA_CMP_LEN, HEAD_DIM, NSA_CMP_HIDDEN), (NSA_CMP_LEN * HEAD_DIM) ** -0.5),
        'nsa_cmp_w2': nrm((n_even, 2, NSA_CMP_HIDDEN, HEAD_DIM), NSA_CMP_HIDDEN ** -0.5),
        'diff_w_in': nrm((n_odd, D_MODEL, DIFF_IN), D_MODEL ** -0.5),
        'diff_w_out': nrm((n_odd, DIFF_OUT, D_MODEL), DIFF_OUT ** -0.5),
        'diff_q_norm': gain((n_odd, DIFF_QK_DIM)),
        'diff_k_norm': gain((n_odd, DIFF_QK_DIM)),
        'diff_lambda': nrm((n_odd, 4, DIFF_QK_DIM), 0.1),
        'diff_out_norm': gain((n_odd, DIFF_V_DIM)),
    }


def reference(x, c, positions, ada_w, ada_b, attn_norm, ffn_norm, ffn_w_gate, ffn_w_up, ffn_w_down,
              sp_w_in, sp_w_out, moba_q_norm, moba_k_norm, nsa_q_norm, nsa_k_norm, nsa_cmp_pos,
              nsa_cmp_w1, nsa_cmp_w2, diff_w_in, diff_w_out, diff_q_norm, diff_k_norm, diff_lambda,
              diff_out_norm):
    cos, sin = rope_tables(positions)
    for i in range(DEPTH):
        mod = jax.nn.silu(c) @ ada_w[i] + ada_b[i]
        sh_a, sc_a, g_a, sh_f, sc_f, g_f = jnp.split(mod[:, None, :], 6, axis=-1)
        h = rms_norm(x, attn_norm[i]) * (1.0 + sc_a) + sh_a
        j = i // 2
        if i % 2 == 0:
            y = sparse_mixer(h, cos, sin, sp_w_in[j], sp_w_out[j], moba_q_norm[j], moba_k_norm[j],
                             nsa_q_norm[j], nsa_k_norm[j], nsa_cmp_pos[j], nsa_cmp_w1[j], nsa_cmp_w2[j])
        else:
            lam_init = 0.8 - 0.6 * float(np.exp(-0.3 * i))
            y = diff_mixer(h, cos, sin, diff_w_in[j], diff_w_out[j], diff_q_norm[j], diff_k_norm[j],
                           diff_lambda[j], diff_out_norm[j], lam_init)
        x = x + g_a * y
        h = rms_norm(x, ffn_norm[i]) * (1.0 + sc_f) + sh_f
        x = x + g_f * swiglu(h, ffn_w_gate[i], ffn_w_up[i], ffn_w_down[i])
    return x
```

```python
import functools

import numpy as np
import jax
import jax.numpy as jnp
from jax import lax
from jax.experimental import pallas as pl
from jax.experimental.pallas import tpu as pltpu

F32 = jnp.float32
BF16 = jnp.bfloat16
HI = lax.Precision.HIGHEST

LANES = 128
HEAD_DIM = 64
ROT_DIM = HEAD_DIM // 4
ROT_HALF = ROT_DIM // 2
ROPE_THETA = 500000.0
NORM_EPS = 1e-6
NEG_INF = -1e30
SCALE = HEAD_DIM ** -0.5

MOBA_HEADS = 8
MOBA_BLOCK = 256
MOBA_TOPK = 3

NSA_HEADS = 8
NSA_KV_HEADS = 2
NSA_GROUP = NSA_HEADS // NSA_KV_HEADS
NSA_CMP_LEN = 32
NSA_CMP_STRIDE = 16
NSA_SEL_BLOCK = 64
NSA_SEL_TOPK = 16
NSA_WINDOW = 512
NSA_FORCE = 1e4

DIFF_HEADS = 8

ATTN_TILE = 256
PROJ_ROWS = 512
FFN_CHUNK = 256
VMEM_LIMIT = 56 * 2 ** 20

NT = (((1,), (1,)), ((), ()))


def _cparams(n_axes):
    return pltpu.CompilerParams(dimension_semantics=("arbitrary",) * n_axes,
                                vmem_limit_bytes=VMEM_LIMIT)


def _const_spec(shape):
    return pl.BlockSpec(shape, lambda *_: (0,) * len(shape))


def _silu(t):
    return t / (1.0 + jnp.exp(-t))


def _sigmoid(t):
    return 1.0 / (1.0 + jnp.exp(-t))


def _lane_iota(rows):
    return lax.broadcasted_iota(jnp.int32, (rows, LANES), 1)


def _mod_body(c_ref, w_ref, b_ref, o_ref):
    a = _silu(c_ref[...])
    o_ref[0] = jnp.dot(a, w_ref[0], precision=HI, preferred_element_type=F32) + b_ref[0]


def _modulation(c, ada_w, ada_b):
    depth, d, n = ada_w.shape
    b = c.shape[0]
    tn = n // 4
    return pl.pallas_call(
        _mod_body,
        out_shape=jax.ShapeDtypeStruct((depth, b, n), F32),
        grid=(depth, n // tn),
        in_specs=[pl.BlockSpec((b, d), lambda i, j: (0, 0)),
                  pl.BlockSpec((1, d, tn), lambda i, j: (i, 0, j)),
                  pl.BlockSpec((1, 1, tn), lambda i, j: (i, 0, j))],
        out_specs=pl.BlockSpec((1, b, tn), lambda i, j: (i, 0, j)),
        compiler_params=_cparams(2),
        name="adaln_modulation",
    )(c, ada_w, ada_b.reshape(depth, 1, n))


def _norm_mod(x, gain, sc, sh):
    ms = jnp.mean(x * x, axis=-1, keepdims=True)
    return x * lax.rsqrt(ms + NORM_EPS) * gain * (1.0 + sc) + sh


def _head_norm_rope(y, gain, cos, sin):
    rows = y.shape[0]
    lane = _lane_iota(rows)
    lo = lane < HEAD_DIM
    y2 = y * y
    s_all = jnp.sum(y2, axis=-1, keepdims=True)
    s_lo = jnp.sum(jnp.where(lo, y2, 0.0), axis=-1, keepdims=True)
    ms = jnp.where(lo, s_lo, s_all - s_lo) * (1.0 / HEAD_DIM)
    yn = y * lax.rsqrt(ms + NORM_EPS) * gain
    ahead = pltpu.roll(yn, LANES - ROT_HALF, 1)
    behind = pltpu.roll(yn, ROT_HALF, 1)
    partner = jnp.where((lane & (HEAD_DIM - 1)) < ROT_HALF, ahead, behind)
    return yn * cos + partner * sin


def _flash_update(state, s, v):
    m, l, acc = state
    m_new = jnp.maximum(m, jnp.max(s, axis=-1, keepdims=True))
    alpha = jnp.exp(m - m_new)
    p = jnp.exp(s - m_new)
    l = alpha * l + jnp.sum(p, axis=-1, keepdims=True)
    acc = alpha * acc + jnp.dot(p.astype(BF16), v, preferred_element_type=F32)
    return m_new, l, acc


def _flash_init(rows):
    return (jnp.full((rows, 1), NEG_INF, F32), jnp.zeros((rows, 1), F32),
            jnp.zeros((rows, LANES), F32))


def _topk_mask(score, valid, k):
    rows, n = score.shape
    idx = lax.broadcasted_iota(jnp.int32, (rows, n), 1)
    rank = jnp.zeros((rows, n), jnp.int32)
    for i in range(n):
        si = score[:, i:i + 1]
        beats = (si > score) | ((si == score) & (idx > i))
        rank = rank + jnp.where(beats, 1, 0)
    return jnp.where((rank < k) & valid, 1.0, 0.0).astype(F32)


def _row_tile(ref, j, rows):
    return ref[0, pl.ds(pl.multiple_of(j * rows, rows), rows), :]


def _sparse_proj_body(x_ref, an_ref, sc_ref, sh_ref, w_ref, cos_ref, sin_ref, hg_ref,
                      mq_ref, mk_ref, mv_ref, km_ref, nq_ref, ck_ref, cv_ref,
                      ks_ref, vs_ref, kw_ref, vw_ref, gt_ref):
    rows = x_ref.shape[0]
    h = _norm_mod(x_ref[...], an_ref[...], sc_ref[0], sh_ref[0]).astype(BF16)
    cos = cos_ref[...]
    sin = sin_ref[...]
    lo = _lane_iota(rows) < HEAD_DIM

    def proj(c0, n):
        return jnp.dot(h, w_ref[:, c0:c0 + n], preferred_element_type=F32)

    def group(y, p):
        return y[:, p * LANES:(p + 1) * LANES]

    def nr(y, gi):
        return _head_norm_rope(y, hg_ref[gi:gi + 1, :], cos, sin)

    def dup(y, ref):
        rolled = pltpu.roll(y, HEAD_DIM, 1)
        ref[0, 0] = jnp.where(lo, y, rolled).astype(ref.dtype)
        ref[0, 1] = jnp.where(lo, rolled, y).astype(ref.dtype)

    moba_w = MOBA_HEADS * HEAD_DIM
    pairs = moba_w // LANES
    y = proj(0, moba_w)
    for p in range(pairs):
        mq_ref[:, p * LANES:(p + 1) * LANES] = nr(group(y, p), 0)
    y = proj(moba_w, moba_w)
    for p in range(pairs):
        k = nr(group(y, p), 1)
        mk_ref[:, p * LANES:(p + 1) * LANES] = k.astype(mk_ref.dtype)
        for r in range(rows // MOBA_BLOCK):
            km_ref[r, :, p * LANES:(p + 1) * LANES] = jnp.mean(
                k[r * MOBA_BLOCK:(r + 1) * MOBA_BLOCK], axis=0, keepdims=True)
    y = proj(2 * moba_w, moba_w)
    mv_ref[...] = y.astype(mv_ref.dtype)
    c0 = 3 * moba_w
    y = proj(c0, NSA_HEADS * HEAD_DIM)
    for p in range(NSA_HEADS * HEAD_DIM // LANES):
        nq_ref[:, p * LANES:(p + 1) * LANES] = nr(group(y, p), 2)
    c0 += NSA_HEADS * HEAD_DIM
    y = proj(c0, 6 * LANES)
    kc = nr(group(y, 0), 3)
    ck_ref[0, 0] = kc[:, :HEAD_DIM]
    ck_ref[0, 1] = kc[:, HEAD_DIM:]
    vc = group(y, 1)
    cv_ref[0, 0] = vc[:, :HEAD_DIM]
    cv_ref[0, 1] = vc[:, HEAD_DIM:]
    dup(nr(group(y, 2), 4), ks_ref)
    dup(group(y, 3), vs_ref)
    dup(nr(group(y, 4), 5), kw_ref)
    dup(group(y, 5), vw_ref)
    c0 += 6 * LANES
    gt_ref[...] = _sigmoid(proj(c0, LANES))


def _sparse_proj(x2, an, sc, sh, w_pad, cos, sin, hgains, batch, seq):
    t, d = x2.shape
    rows = min(PROJ_ROWS, seq)
    per_b = seq // rows
    nb = seq // MOBA_BLOCK
    g = NSA_KV_HEADS
    w512 = MOBA_HEADS * HEAD_DIM
    row_spec = lambda w: pl.BlockSpec((rows, w), lambda i: (i, 0))
    mod_spec = pl.BlockSpec((1, 1, d), lambda i: (i // per_b, 0, 0))
    grp64 = pl.BlockSpec((1, g, rows, HEAD_DIM), lambda i: (i // per_b, 0, i % per_b, 0))
    grp128 = pl.BlockSpec((1, g, rows, LANES), lambda i: (i // per_b, 0, i % per_b, 0))
    out_shape = [
        jax.ShapeDtypeStruct((t, w512), F32),
        jax.ShapeDtypeStruct((t, w512), BF16),
        jax.ShapeDtypeStruct((t, w512), BF16),
        jax.ShapeDtypeStruct((batch * nb, 1, w512), F32),
        jax.ShapeDtypeStruct((t, w512), F32),
        jax.ShapeDtypeStruct((batch, g, seq, HEAD_DIM), F32),
        jax.ShapeDtypeStruct((batch, g, seq, HEAD_DIM), F32),
        jax.ShapeDtypeStruct((batch, g, seq, LANES), BF16),
        jax.ShapeDtypeStruct((batch, g, seq, LANES), BF16),
        jax.ShapeDtypeStruct((batch, g, seq, LANES), BF16),
        jax.ShapeDtypeStruct((batch, g, seq, LANES), BF16),
        jax.ShapeDtypeStruct((t, LANES), F32),
    ]
    out_specs = [row_spec(w512), row_spec(w512), row_spec(w512),
                 pl.BlockSpec((rows // MOBA_BLOCK, 1, w512), lambda i: (i, 0, 0)),
                 row_spec(w512), grp64, grp64, grp128, grp128, grp128, grp128, row_spec(LANES)]
    return pl.pallas_call(
        _sparse_proj_body,
        out_shape=out_shape,
        grid=(t // rows,),
        in_specs=[row_spec(d), _const_spec((1, d)), mod_spec, mod_spec,
                  _const_spec(w_pad.shape), row_spec(LANES), row_spec(LANES),
                  _const_spec(hgains.shape)],
        out_specs=out_specs,
        compiler_params=_cparams(1),
        name="sparse_in_proj",
    )(x2, an, sc, sh, w_pad, cos, sin, hgains)


def _moba_body(q_ref, k_ref, v_ref, km_ref, o_ref):
    qi = pl.program_id(2)
    tq = q_ref.shape[1]
    nb = km_ref.shape[1]
    q = q_ref[0]
    km = km_ref[0]
    lo = _lane_iota(tq) < HEAD_DIM
    blk = lax.broadcasted_iota(jnp.int32, (tq, nb), 1)
    past = blk < qi
    row = lax.broadcasted_iota(jnp.int32, (tq, tq), 0)
    col = lax.broadcasted_iota(jnp.int32, (tq, tq), 1)
    causal = col <= row
    k_own = _row_tile(k_ref, qi, tq)
    v_own = _row_tile(v_ref, qi, tq)
    outs = []
    for half in (lo, ~lo):
        qm = jnp.where(half, q, 0.0)
        gate = lax.dot_general(qm, km, NT, precision=HI, preferred_element_type=F32)
        sel = _topk_mask(jnp.where(past, gate, NEG_INF), past, min(MOBA_TOPK, nb))
        qb = (qm * SCALE).astype(BF16)
        s = lax.dot_general(qb, k_own, NT, preferred_element_type=F32)
        state = _flash_update(_flash_init(tq), jnp.where(causal, s, NEG_INF), v_own)

        def body(j, st, qb=qb, sel=sel):
            s = lax.dot_general(qb, _row_tile(k_ref, j, tq), NT, preferred_element_type=F32)
            chosen = jnp.sum(jnp.where(blk == j, sel, 0.0), axis=-1, keepdims=True) > 0.5
            return _flash_update(st, jnp.where(chosen, s, NEG_INF), _row_tile(v_ref, j, tq))

        _, l, acc = lax.fori_loop(0, qi, body, state)
        outs.append(acc / l)
    o_ref[0] = jnp.where(lo, outs[0], outs[1]).astype(o_ref.dtype)


def _moba_attention(mq, mk, mv, km):
    batch, seq, w = mq.shape
    tq = MOBA_BLOCK
    nb = seq // tq
    return pl.pallas_call(
        _moba_body,
        out_shape=jax.ShapeDtypeStruct((batch, seq, w), BF16),
        grid=(batch, w // LANES, nb),
        in_specs=[pl.BlockSpec((1, tq, LANES), lambda b, p, i: (b, i, p)),
                  pl.BlockSpec((1, seq, LANES), lambda b, p, i: (b, 0, p)),
                  pl.BlockSpec((1, seq, LANES), lambda b, p, i: (b, 0, p)),
                  pl.BlockSpec((1, nb, LANES), lambda b, p, i: (b, 0, p))],
        out_specs=pl.BlockSpec((1, tq, LANES), lambda b, p, i: (b, i, p)),
        compiler_params=_cparams(3),
        name="moba_attention",
    )(mq, mk, mv, km)


def _compress_body(t_ref, pos_ref, w1a_ref, w1b_ref, w2_ref, o_ref):
    t = t_ref[0]
    r = t.shape[0]
    a = jnp.dot(t + pos_ref[0:1, :], w1a_ref[...], precision=HI, preferred_element_type=F32)
    b = jnp.dot(t + pos_ref[1:2, :], w1b_ref[...], precision=HI, preferred_element_type=F32)
    hid = _silu(a + pltpu.roll(b, r - 1, 0))
    out = jnp.dot(hid, w2_ref[...], precision=HI, preferred_element_type=F32)
    rowi = lax.broadcasted_iota(jnp.int32, out.shape, 0)
    o_ref[0] = jnp.where(rowi < r - 1, out, 0.0)


def _compress(t, pos, w1, w2):
    n, r, w = t.shape
    half = NSA_CMP_LEN // 2
    hidden = w1.shape[-1]
    w1a = w1[:half].reshape(w, hidden)
    w1b = w1[half:].reshape(w, hidden)
    w2d = jnp.concatenate([w2, w2], axis=1)
    return pl.pallas_call(
        _compress_body,
        out_shape=jax.ShapeDtypeStruct((n, r, LANES), F32),
        grid=(n,),
        in_specs=[pl.BlockSpec((1, r, w), lambda i: (i, 0, 0)), _const_spec((2, w)),
                  _const_spec((w, hidden)), _const_spec((w, hidden)), _const_spec((hidden, LANES))],
        out_specs=pl.BlockSpec((1, r, LANES), lambda i: (i, 0, 0)),
        compiler_params=_cparams(1),
        name="nsa_compress",
    )(t, pos.reshape(2, w), w1a, w1b, w2d)


def _nsa_cmp_body(q_ref, kc_ref, vc_ref, ov_ref, oc_ref, sel_ref):
    qi = pl.program_id(2)
    tq = q_ref.shape[1]
    r = kc_ref.shape[1]
    nsel = ov_ref.shape[1]
    kc = kc_ref[0]
    vc = vc_ref[0]
    lo = _lane_iota(tq) < HEAD_DIM
    pos = qi * tq + lax.broadcasted_iota(jnp.int32, (tq, r), 0)
    n = lax.broadcasted_iota(jnp.int32, (tq, r), 1)
    ok = (n * NSA_CMP_STRIDE + (NSA_CMP_LEN - 1) <= pos) & (n < r - 1)
    psum = jnp.zeros((tq, r), F32)
    outs = []
    for hd in range(NSA_GROUP):
        qp = q_ref[0, :, (hd // 2) * LANES:(hd // 2 + 1) * LANES]
        qm = jnp.where(lo if hd % 2 == 0 else ~lo, qp, 0.0)
        s = lax.dot_general(qm, kc, NT, precision=HI, preferred_element_type=F32) * SCALE
        s = jnp.where(ok, s, NEG_INF)
        e = jnp.where(ok, jnp.exp(s - jnp.max(s, axis=-1, keepdims=True)), 0.0)
        l = jnp.sum(e, axis=-1, keepdims=True)
        p = e * jnp.where(l > 0.0, 1.0 / l, 0.0)
        psum = psum + p
        outs.append(jnp.dot(p, vc, precision=HI, preferred_element_type=F32))
    for pr in range(NSA_GROUP // 2):
        oc_ref[0, :, pr * LANES:(pr + 1) * LANES] = jnp.where(lo, outs[2 * pr], outs[2 * pr + 1])
    imp = jnp.dot(psum, ov_ref[...], precision=HI, preferred_element_type=F32)
    blk = lax.broadcasted_iota(jnp.int32, (tq, nsel), 1)
    cur = (qi * tq + lax.broadcasted_iota(jnp.int32, (tq, nsel), 0)) // NSA_SEL_BLOCK
    ok_blk = blk <= cur
    forced = ok_blk & ((blk == 0) | (blk >= cur - 1))
    score = jnp.where(ok_blk, jnp.where(forced, NSA_FORCE, imp), NEG_INF)
    sel_ref[0, 0] = _topk_mask(score, ok_blk, min(NSA_SEL_TOPK, nsel))


def _nsa_overlap(r, nsel):
    cs = np.arange(r) * NSA_CMP_STRIDE
    ss = np.arange(nsel) * NSA_SEL_BLOCK
    ov = (np.minimum(cs[:, None] + NSA_CMP_LEN, ss[None, :] + NSA_SEL_BLOCK)
          - np.maximum(cs[:, None], ss[None, :]))
    return jnp.asarray(np.clip(ov, 0, None) / NSA_CMP_LEN, dtype=F32)


def _nsa_cmp_attention(nq, kc, vc, batch, seq):
    g = NSA_KV_HEADS
    tq = ATTN_TILE
    r = kc.shape[1]
    nsel = seq // NSA_SEL_BLOCK
    gw = NSA_GROUP * HEAD_DIM
    return pl.pallas_call(
        _nsa_cmp_body,
        out_shape=[jax.ShapeDtypeStruct((batch, seq, g * gw), F32),
                   jax.ShapeDtypeStruct((batch, g, seq, nsel), F32)],
        grid=(batch, g, seq // tq),
        in_specs=[pl.BlockSpec((1, tq, gw), lambda b, gi, i: (b, i, gi)),
                  pl.BlockSpec((1, r, LANES), lambda b, gi, i: (b * g + gi, 0, 0)),
                  pl.BlockSpec((1, r, LANES), lambda b, gi, i: (b * g + gi, 0, 0)),
                  _const_spec((r, nsel))],
        out_specs=[pl.BlockSpec((1, tq, gw), lambda b, gi, i: (b, i, gi)),
                   pl.BlockSpec((1, 1, tq, nsel), lambda b, gi, i: (b, gi, i, 0))],
        compiler_params=_cparams(3),
        name="nsa_compressed_attention",
    )(nq, kc, vc, _nsa_overlap(r, nsel))


def _nsa_main_body(q_ref, ks_ref, vs_ref, kw_ref, vw_ref, sel_ref, gt_ref, oc_ref, o_ref):
    gi = pl.program_id(1)
    qi = pl.program_id(2)
    tq = q_ref.shape[1]
    nsel = sel_ref.shape[3]
    lane = _lane_iota(tq)
    lo = lane < HEAD_DIM
    row = lax.broadcasted_iota(jnp.int32, (tq, tq), 0)
    col = lax.broadcasted_iota(jnp.int32, (tq, tq), 1)
    sel = sel_ref[0, 0].astype(BF16)
    em = lax.broadcasted_iota(jnp.int32, (nsel, tq), 0)
    et = lax.broadcasted_iota(jnp.int32, (nsel, tq), 1)
    qb = []
    for hd in range(NSA_GROUP):
        qp = q_ref[0, :, (hd // 2) * LANES:(hd // 2 + 1) * LANES]
        qb.append((jnp.where(lo if hd % 2 == 0 else ~lo, qp, 0.0) * SCALE).astype(BF16))

    def sel_mask(j):
        expand = jnp.where(em == (j * tq + et) // NSA_SEL_BLOCK, 1.0, 0.0).astype(BF16)
        return jnp.dot(sel, expand, preferred_element_type=F32) > 0.5

    def step(states, k, v, mask):
        out = []
        for hd in range(NSA_GROUP):
            s = lax.dot_general(qb[hd], k, NT, preferred_element_type=F32)
            out.append(_flash_update(states[hd], jnp.where(mask, s, NEG_INF), v))
        return tuple(out)

    init = tuple(_flash_init(tq) for _ in range(NSA_GROUP))
    causal = col <= row

    st = step(init, _row_tile(ks_ref.at[0], qi, tq), _row_tile(vs_ref.at[0], qi, tq),
              sel_mask(qi) & causal)
    st = lax.fori_loop(
        0, qi,
        lambda j, c: step(c, _row_tile(ks_ref.at[0], j, tq), _row_tile(vs_ref.at[0], j, tq),
                          sel_mask(j)),
        st)
    o_sel = [acc / l for (_, l, acc) in st]

    st = init
    for d in range(NSA_WINDOW // tq + 1):
        j = jnp.maximum(qi - d, 0)
        delta = row + d * tq - col
        width = jnp.where(qi - d >= 0, NSA_WINDOW, 0)
        mask = (delta >= 0) & (delta < width)
        st = step(st, _row_tile(kw_ref.at[0], j, tq), _row_tile(vw_ref.at[0], j, tq), mask)
    o_win = [acc / l for (_, l, acc) in st]

    gt = gt_ref[0]

    def gate(branch, hd):
        c = branch * NSA_HEADS + gi * NSA_GROUP + hd
        return jnp.sum(jnp.where(lane == c, gt, 0.0), axis=-1, keepdims=True)

    for pr in range(NSA_GROUP // 2):
        a, b = 2 * pr, 2 * pr + 1
        o = (jnp.where(lo, gate(0, a), gate(0, b)) * oc_ref[0, :, pr * LANES:(pr + 1) * LANES]
             + jnp.where(lo, gate(1, a) * o_sel[a], gate(1, b) * o_sel[b])
             + jnp.where(lo, gate(2, a) * o_win[a], gate(2, b) * o_win[b]))
        o_ref[0, :, pr * LANES:(pr + 1) * LANES] = o.astype(o_ref.dtype)


def _nsa_main_attention(nq, ks, vs, kw, vw, sel, gates, oc):
    batch, seq, w = nq.shape
    g = NSA_KV_HEADS
    tq = ATTN_TILE
    gw = NSA_GROUP * HEAD_DIM
    nsel = sel.shape[3]
    kv_spec = pl.BlockSpec((1, 1, seq, LANES), lambda b, gi, i: (b, gi, 0, 0))
    q_spec = pl.BlockSpec((1, tq, gw), lambda b, gi, i: (b, i, gi))
    return pl.pallas_call(
        _nsa_main_body,
        out_shape=jax.ShapeDtypeStruct((batch, seq, w), BF16),
        grid=(batch, g, seq // tq),
        in_specs=[q_spec, kv_spec, kv_spec, kv_spec, kv_spec,
                  pl.BlockSpec((1, 1, tq, nsel), lambda b, gi, i: (b, gi, i, 0)),
                  pl.BlockSpec((1, tq, LANES), lambda b, gi, i: (b, i, 0)),
                  q_spec],
        out_specs=q_spec,
        compiler_params=_cparams(3),
        name="nsa_selected_window_attention",
    )(nq, ks, vs, kw, vw, sel, gates, oc)


def _diff_proj_body(x_ref, an_ref, sc_ref, sh_ref, w_ref, cos_ref, sin_ref, hg_ref,
                    q_ref, k_ref, v_ref):
    h = _norm_mod(x_ref[...], an_ref[...], sc_ref[0], sh_ref[0]).astype(BF16)
    cos = cos_ref[...]
    sin = sin_ref[...]
    w = q_ref.shape[1]
    chunk = 4 * LANES
    for c0 in range(0, w, chunk):
        yq = jnp.dot(h, w_ref[:, c0:c0 + chunk], preferred_element_type=F32)
        yk = jnp.dot(h, w_ref[:, w + c0:w + c0 + chunk], preferred_element_type=F32)
        yv = jnp.dot(h, w_ref[:, 2 * w + c0:2 * w + c0 + chunk], preferred_element_type=F32)
        for p in range(chunk // LANES):
            sl = slice(p * LANES, (p + 1) * LANES)
            dst = slice(c0 + p * LANES, c0 + (p + 1) * LANES)
            q_ref[:, dst] = (_head_norm_rope(yq[:, sl], hg_ref[0:1, :], cos, sin) * SCALE).astype(q_ref.dtype)
            k_ref[:, dst] = _head_norm_rope(yk[:, sl], hg_ref[1:2, :], cos, sin).astype(k_ref.dtype)
        v_ref[:, c0:c0 + chunk] = yv.astype(v_ref.dtype)


def _diff_proj(x2, an, sc, sh, w, cos, sin, hgains, seq):
    t, d = x2.shape
    rows = min(PROJ_ROWS, seq)
    per_b = seq // rows
    wq = w.shape[1] // 3
    row_spec = lambda n: pl.BlockSpec((rows, n), lambda i: (i, 0))
    mod_spec = pl.BlockSpec((1, 1, d), lambda i: (i // per_b, 0, 0))
    return pl.pallas_call(
        _diff_proj_body,
        out_shape=[jax.ShapeDtypeStruct((t, wq), BF16)] * 3,
        grid=(t // rows,),
        in_specs=[row_spec(d), _const_spec((1, d)), mod_spec, mod_spec, _const_spec(w.shape),
                  row_spec(LANES), row_spec(LANES), _const_spec(hgains.shape)],
        out_specs=[row_spec(wq)] * 3,
        compiler_params=_cparams(1),
        name="diff_in_proj",
    )(x2, an, sc, sh, w, cos, sin, hgains)


def _diff_body(lam_ref, on_ref, q_ref, k_ref, v_ref, o_ref, *, lam_init):
    qi = pl.program_id(2)
    tq = q_ref.shape[1]
    lo = _lane_iota(tq) < HEAD_DIM
    row = lax.broadcasted_iota(jnp.int32, (tq, tq), 0)
    col = lax.broadcasted_iota(jnp.int32, (tq, tq), 1)
    causal = col <= row
    q = q_ref[0]
    zero = jnp.zeros_like(q)
    qb = (jnp.where(lo, q, zero), jnp.where(lo, zero, q))

    def step(states, j, mask):
        k = _row_tile(k_ref, j, tq)
        v = _row_tile(v_ref, j, tq)
        out = []
        for c in range(2):
            s = lax.dot_general(qb[c], k, NT, preferred_element_type=F32)
            if mask is not None:
                s = jnp.where(mask, s, NEG_INF)
            out.append(_flash_update(states[c], s, v))
        return tuple(out)

    st = step((_flash_init(tq), _flash_init(tq)), qi, causal)
    st = lax.fori_loop(0, qi, lambda j, c: step(c, j, None), st)
    lp = lam_ref[...]
    lam = (jnp.exp(jnp.sum(lp[0:1] * lp[1:2], axis=-1, keepdims=True))
           - jnp.exp(jnp.sum(lp[2:3] * lp[3:4], axis=-1, keepdims=True)) + lam_init)
    o = st[0][2] / st[0][1] - lam * (st[1][2] / st[1][1])
    ms = jnp.mean(o * o, axis=-1, keepdims=True)
    o = o * lax.rsqrt(ms + NORM_EPS) * on_ref[...] * (1.0 - lam_init)
    o_ref[0] = o.astype(o_ref.dtype)


def _diff_attention(q, k, v, lam_params, out_norm, lam_init):
    batch, seq, w = q.shape
    tq = ATTN_TILE
    kv_spec = pl.BlockSpec((1, seq, LANES), lambda b, h, i: (b, 0, h))
    q_spec = pl.BlockSpec((1, tq, LANES), lambda b, h, i: (b, i, h))
    return pl.pallas_call(
        functools.partial(_diff_body, lam_init=lam_init),
        out_shape=jax.ShapeDtypeStruct((batch, seq, w), BF16),
        grid=(batch, w // LANES, seq // tq),
        in_specs=[_const_spec(lam_params.shape), _const_spec((1, LANES)), q_spec, kv_spec, kv_spec],
        out_specs=q_spec,
        compiler_params=_cparams(3),
        name="diff_attention",
    )(lam_params, out_norm.reshape(1, LANES), q, k, v)


def _out_ffn_body(*refs, n_attn):
    x_ref = refs[0]
    attn = refs[1:1 + 2 * n_attn]
    ga_ref, fn_ref, sc_ref, sh_ref, gf_ref, wg_ref, wu_ref, wd_ref, o_ref = refs[1 + 2 * n_attn:]
    y = None
    for a in range(n_attn):
        part = jnp.dot(attn[2 * a][...], attn[2 * a + 1][...], preferred_element_type=F32)
        y = part if y is None else y + part
    x1 = x_ref[...] + ga_ref[0] * y
    h = _norm_mod(x1, fn_ref[...], sc_ref[0], sh_ref[0]).astype(BF16)
    acc = jnp.zeros(x1.shape, F32)
    for c0 in range(0, wg_ref.shape[1], FFN_CHUNK):
        gch = jnp.dot(h, wg_ref[:, c0:c0 + FFN_CHUNK], preferred_element_type=F32)
        uch = jnp.dot(h, wu_ref[:, c0:c0 + FFN_CHUNK], preferred_element_type=F32)
        act = (_silu(gch) * uch).astype(BF16)
        acc = acc + jnp.dot(act, wd_ref[c0:c0 + FFN_CHUNK, :], preferred_element_type=F32)
    o_ref[...] = x1 + gf_ref[0] * acc


def _out_ffn(x2, attn_pairs, ga, fn, sc, sh, gf, wg, wu, wd, seq):
    t, d = x2.shape
    rows = min(PROJ_ROWS, seq)
    per_b = seq // rows
    row_spec = lambda n: pl.BlockSpec((rows, n), lambda i: (i, 0))
    mod_spec = pl.BlockSpec((1, 1, d), lambda i: (i // per_b, 0, 0))
    resident = lambda a: pl.BlockSpec(a.shape, lambda i: (0, 0), pipeline_mode=pl.Buffered(1))
    args, specs = [x2], [row_spec(d)]
    for o, w in attn_pairs:
        args += [o, w]
        specs += [row_spec(o.shape[1]), resident(w)]
    args += [ga, fn, sc, sh, gf, wg, wu, wd]
    specs += [mod_spec, _const_spec((1, d)), mod_spec, mod_spec, mod_spec,
              resident(wg), resident(wu), resident(wd)]
    return pl.pallas_call(
        functools.partial(_out_ffn_body, n_attn=len(attn_pairs)),
        out_shape=jax.ShapeDtypeStruct((t, d), F32),
        grid=(t // rows,),
        in_specs=specs,
        out_specs=row_spec(d),
        compiler_params=_cparams(1),
        name="out_proj_swiglu",
    )(*args)


def _rope_lane_tables(positions):
    inv_freq = 1.0 / (ROPE_THETA ** (jnp.arange(0, ROT_DIM, 2, dtype=F32) / ROT_DIM))
    ang = positions.astype(F32).reshape(-1, 1) * inv_freq
    cos, sin = jnp.cos(ang), jnp.sin(ang)
    rest = HEAD_DIM - ROT_DIM
    ones = jnp.ones((ang.shape[0], rest), F32)
    zeros = jnp.zeros((ang.shape[0], rest), F32)
    cos_h = jnp.concatenate([cos, cos, ones], axis=1)
    sin_h = jnp.concatenate([-sin, sin, zeros], axis=1)
    return jnp.concatenate([cos_h, cos_h], axis=1), jnp.concatenate([sin_h, sin_h], axis=1)


def _pair_gain(gain):
    return jnp.concatenate([gain, gain]).reshape(1, LANES)


def _pad_rows(a, rows=8):
    return jnp.concatenate([a, jnp.zeros((rows - a.shape[0], a.shape[1]), a.dtype)], axis=0)


def kernel(x, c, positions, ada_w, ada_b, attn_norm, ffn_norm, ffn_w_gate, ffn_w_up, ffn_w_down, sp_w_in, sp_w_out, moba_q_norm, moba_k_norm, nsa_q_norm, nsa_k_norm, nsa_cmp_pos, nsa_cmp_w1, nsa_cmp_w2, diff_w_in, diff_w_out, diff_q_norm, diff_k_norm, diff_lambda, diff_out_norm):
    batch, seq, d = x.shape
    depth = ada_w.shape[0]
    assert seq % MOBA_BLOCK == 0 and seq % ATTN_TILE == 0 and NSA_WINDOW % ATTN_TILE == 0
    g = NSA_KV_HEADS
    cos, sin = _rope_lane_tables(positions)
    mod = _modulation(c, ada_w, ada_b).reshape(depth, batch, 6, 1, d)
    x2 = x.reshape(batch * seq, d)
    for i in range(depth):
        sh_a, sc_a, g_a, sh_f, sc_f, g_f = (mod[i, :, n] for n in range(6))
        an = attn_norm[i].reshape(1, d)
        j = i // 2
        if i % 2 == 0:
            w_in = sp_w_in[j]
            pad = (-w_in.shape[1]) % LANES
            w_pad = jnp.concatenate([w_in, jnp.zeros((d, pad), w_in.dtype)], axis=1).astype(BF16)
            hgains = _pad_rows(jnp.concatenate(
                [_pair_gain(moba_q_norm[j]), _pair_gain(moba_k_norm[j]), _pair_gain(nsa_q_norm[j])]
                + [_pair_gain(nsa_k_norm[j, n]) for n in range(3)], axis=0))
            (mq, mk, mv, km, nq, ck, cv, ks, vs, kw, vw, gates) = _sparse_proj(
                x2, an, sc_a, sh_a, w_pad, cos, sin, hgains, batch, seq)
            w512 = MOBA_HEADS * HEAD_DIM
            to3 = lambda a: a.reshape(batch, seq, a.shape[-1])
            o_moba = _moba_attention(to3(mq), to3(mk), to3(mv),
                                     km.reshape(batch, seq // MOBA_BLOCK, w512))
            r = seq // NSA_CMP_STRIDE
            cw = NSA_CMP_STRIDE * HEAD_DIM
            kc = _compress(ck.reshape(batch * g, r, cw), nsa_cmp_pos[j, 0], nsa_cmp_w1[j, 0], nsa_cmp_w2[j, 0])
            vc = _compress(cv.reshape(batch * g, r, cw), nsa_cmp_pos[j, 1], nsa_cmp_w1[j, 1], nsa_cmp_w2[j, 1])
            oc, sel = _nsa_cmp_attention(to3(nq), kc, vc, batch, seq)
            o_nsa = _nsa_main_attention(to3(nq), ks, vs, kw, vw, sel, to3(gates), oc)
            w_out = sp_w_out[j].astype(BF16)
            attn_pairs = [(o_moba.reshape(batch * seq, w512), w_out[:w512]),
                          (o_nsa.reshape(batch * seq, w512), w_out[w512:])]
        else:
            lam_init = 0.8 - 0.6 * float(np.exp(-0.3 * i))
            hgains = _pad_rows(jnp.concatenate(
                [_pair_gain(diff_q_norm[j]), _pair_gain(diff_k_norm[j])], axis=0))
            q, k, v = _diff_proj(x2, an, sc_a, sh_a, diff_w_in[j].astype(BF16), cos, sin, hgains, seq)
            to3 = lambda a: a.reshape(batch, seq, a.shape[-1])
            o = _diff_attention(to3(q), to3(k), to3(v), diff_lambda[j], diff_out_norm[j], lam_init)
            attn_pairs = [(o.reshape(batch * seq, o.shape[-1]), diff_w_out[j].astype(BF16))]
        x2 = _out_ffn(x2, attn_pairs, g_a, ffn_norm[i].reshape(1, d), sc_f, sh_f, g_f,
                      ffn_w_gate[i].astype(BF16), ffn_w_up[i].astype(BF16),
                      ffn_w_down[i].astype(BF16), seq)
    return x2.reshape(batch, seq, d)
```

```python
import functools

import numpy as np
import jax
import jax.numpy as jnp
from jax import lax
from jax.experimental import pallas as pl
from jax.experimental.pallas import tpu as pltpu

F32 = jnp.float32
BF16 = jnp.bfloat16
HI = lax.Precision.HIGHEST

LANES = 128
HEAD_DIM = 64
ROT_DIM = HEAD_DIM // 4
ROT_HALF = ROT_DIM // 2
ROPE_THETA = 500000.0
NORM_EPS = 1e-6
NEG_INF = -1e30
M_INIT = -1e20
SCALE = HEAD_DIM ** -0.5
QK_SCALE = SCALE * 1.4426950408889634

MOBA_HEADS = 8
MOBA_BLOCK = 256
MOBA_TOPK = 3

NSA_HEADS = 8
NSA_KV_HEADS = 2
NSA_GROUP = NSA_HEADS // NSA_KV_HEADS
NSA_CMP_LEN = 32
NSA_CMP_STRIDE = 16
NSA_SEL_BLOCK = 64
NSA_SEL_TOPK = 16
NSA_WINDOW = 512
NSA_FORCE = 1e4

DIFF_HEADS = 8

Q_TILE = 512
K_TILE = 1024
TAIL_TILE = 512
CMP_TILE = 256
SUBLANES = 8
PROJ_ROWS = 512
FFN_CHUNK = 256
VMEM_LIMIT = 56 * 2 ** 20

NT = (((1,), (1,)), ((), ()))


def _cparams(n_axes):
    return pltpu.CompilerParams(dimension_semantics=("arbitrary",) * n_axes,
                                vmem_limit_bytes=VMEM_LIMIT)


def _const_spec(shape):
    return pl.BlockSpec(shape, lambda *_: (0,) * len(shape))


def _silu(t):
    return t / (1.0 + jnp.exp(-t))


def _sigmoid(t):
    return 1.0 / (1.0 + jnp.exp(-t))


def _lane_iota(rows):
    return lax.broadcasted_iota(jnp.int32, (rows, LANES), 1)


def _mod_body(c_ref, w_ref, b_ref, o_ref):
    a = _silu(c_ref[...])
    o_ref[0] = jnp.dot(a, w_ref[0], precision=HI, preferred_element_type=F32) + b_ref[0]


def _modulation(c, ada_w, ada_b):
    depth, d, n = ada_w.shape
    b = c.shape[0]
    tn = n // 4
    return pl.pallas_call(
        _mod_body,
        out_shape=jax.ShapeDtypeStruct((depth, b, n), F32),
        grid=(depth, n // tn),
        in_specs=[pl.BlockSpec((b, d), lambda i, j: (0, 0)),
                  pl.BlockSpec((1, d, tn), lambda i, j: (i, 0, j)),
                  pl.BlockSpec((1, 1, tn), lambda i, j: (i, 0, j))],
        out_specs=pl.BlockSpec((1, b, tn), lambda i, j: (i, 0, j)),
        compiler_params=_cparams(2),
        name="adaln_modulation",
    )(c, ada_w, ada_b.reshape(depth, 1, n))


def _norm_mod(x, gain, sc, sh):
    ms = jnp.mean(x * x, axis=-1, keepdims=True)
    return x * lax.rsqrt(ms + NORM_EPS) * gain * (1.0 + sc) + sh


def _head_norm_rope(y, gain, cos, sin):
    rows = y.shape[0]
    lane = _lane_iota(rows)
    lo = lane < HEAD_DIM
    y2 = y * y
    s_all = jnp.sum(y2, axis=-1, keepdims=True)
    s_lo = jnp.sum(jnp.where(lo, y2, 0.0), axis=-1, keepdims=True)
    ms = jnp.where(lo, s_lo, s_all - s_lo) * (1.0 / HEAD_DIM)
    yn = y * lax.rsqrt(ms + NORM_EPS) * gain
    ahead = pltpu.roll(yn, LANES - ROT_HALF, 1)
    behind = pltpu.roll(yn, ROT_HALF, 1)
    partner = jnp.where((lane & (HEAD_DIM - 1)) < ROT_HALF, ahead, behind)
    return yn * cos + partner * sin


def _softmax_step(m, s):
    m_new = jnp.maximum(m, jnp.max(s, axis=-1, keepdims=True))
    return m_new, jnp.exp2(m - m_new), jnp.exp2(s - m_new)


def _accumulate(acc_ref, alpha, p, v):
    acc_ref[...] = alpha * acc_ref[...] + jnp.dot(p.astype(BF16), v, preferred_element_type=F32)


def _normalized(acc):
    return acc / pltpu.roll(acc, HEAD_DIM, 1)


def _topk_rows(score, k):
    n, cols = score.shape
    nblk = n // SUBLANES
    blocks = [score[b * SUBLANES:(b + 1) * SUBLANES] for b in range(nblk)]
    sub = lax.broadcasted_iota(jnp.int32, (SUBLANES, cols), 0)
    ranks = [jnp.zeros((SUBLANES, cols), jnp.int32) for _ in range(nblk)]
    for i in range(n):
        si = score[i:i + 1, :]
        for b in range(nblk):
            if b > i // SUBLANES:
                beats = si >= blocks[b]
            elif b < i // SUBLANES:
                beats = si > blocks[b]
            else:
                beats = (si > blocks[b]) | ((si == blocks[b]) & (sub > i % SUBLANES))
            ranks[b] = ranks[b] + jnp.where(beats, 1, 0)
    return jnp.concatenate(ranks, axis=0) < k


def _lane_spread(allowed):
    n, cols = allowed.shape
    parts = [allowed, allowed]
    if n < HEAD_DIM:
        fill = jnp.zeros((HEAD_DIM - n, cols), F32)
        parts = [allowed, fill, allowed, fill]
    return jnp.where(jnp.concatenate(parts, axis=0).T > 0.5, 0.0, NEG_INF)


def _row_tile(ref, j, rows):
    return ref[0, pl.ds(pl.multiple_of(j * rows, rows), rows), :]


def _causal_sweep(qi, tq, kt, tile, state):
    n_wide = (qi * tq) // kt
    state = lax.fori_loop(0, n_wide, lambda j, c: tile(j, c, kt, None), state)
    row = lax.broadcasted_iota(jnp.int32, (tq, tq), 0)
    col = lax.broadcasted_iota(jnp.int32, (tq, tq), 1)
    return lax.fori_loop(n_wide * (kt // tq), qi + 1,
                         lambda j, c: tile(j, c, tq, col - row <= (qi - j) * tq), state)


def _sparse_proj_body(x_ref, an_ref, sc_ref, sh_ref, w_ref, cos_ref, sin_ref, hg_ref,
                      mq_ref, mka_ref, mkb_ref, mva_ref, mvb_ref, km_ref, nq_ref, ck_ref, cv_ref,
                      ksa_ref, ksb_ref, vsa_ref, vsb_ref, kw_ref, vwa_ref, vwb_ref, gt_ref,
                      *, per_b):
    rows = x_ref.shape[0]
    h = _norm_mod(x_ref[...], an_ref[...], sc_ref[0], sh_ref[0]).astype(BF16)
    cos = cos_ref[...]
    sin = sin_ref[...]
    lane = _lane_iota(rows)
    lo = lane < HEAD_DIM
    seq_pos = (pl.program_id(0) % per_b) * rows + lax.broadcasted_iota(jnp.int32, (rows, LANES), 0)
    lane_blk = lane & (HEAD_DIM - 1)
    moba_hot = jnp.where(lane_blk == seq_pos // MOBA_BLOCK, 1.0, 0.0)
    sel_hot = jnp.where(lane_blk == seq_pos // NSA_SEL_BLOCK, 1.0, 0.0)

    def proj(c0, n):
        return jnp.dot(h, w_ref[:, c0:c0 + n], preferred_element_type=F32)

    def group(y, p):
        return y[:, p * LANES:(p + 1) * LANES]

    def nr(y, gi):
        return _head_norm_rope(y, hg_ref[gi:gi + 1, :], cos, sin)

    def split_groups(y, fill, a_ref, b_ref):
        rolled = pltpu.roll(y, HEAD_DIM, 1)
        a_ref[0, 0] = jnp.where(lo, y, fill).astype(a_ref.dtype)
        b_ref[0, 0] = jnp.where(lo, fill, rolled).astype(b_ref.dtype)
        a_ref[0, 1] = jnp.where(lo, rolled, fill).astype(a_ref.dtype)
        b_ref[0, 1] = jnp.where(lo, fill, y).astype(b_ref.dtype)

    moba_w = MOBA_HEADS * HEAD_DIM
    pairs = moba_w // LANES
    y = proj(0, moba_w)
    for p in range(pairs):
        mq_ref[:, p * LANES:(p + 1) * LANES] = nr(group(y, p), 0)
    y = proj(moba_w, moba_w)
    for p in range(pairs):
        sl = slice(p * LANES, (p + 1) * LANES)
        k = nr(group(y, p), 1)
        mka_ref[:, sl] = jnp.where(lo, k, moba_hot).astype(mka_ref.dtype)
        mkb_ref[:, sl] = jnp.where(lo, moba_hot, k).astype(mkb_ref.dtype)
        for r in range(rows // MOBA_BLOCK):
            km_ref[r, :, sl] = jnp.mean(k[r * MOBA_BLOCK:(r + 1) * MOBA_BLOCK], axis=0, keepdims=True)
    y = proj(2 * moba_w, moba_w)
    for p in range(pairs):
        sl = slice(p * LANES, (p + 1) * LANES)
        mva_ref[:, sl] = jnp.where(lo, group(y, p), 1.0).astype(mva_ref.dtype)
        mvb_ref[:, sl] = jnp.where(lo, 1.0, group(y, p)).astype(mvb_ref.dtype)
    c0 = 3 * moba_w
    y = proj(c0, NSA_HEADS * HEAD_DIM)
    for p in range(NSA_HEADS * HEAD_DIM // LANES):
        nq_ref[:, p * LANES:(p + 1) * LANES] = nr(group(y, p), 2)
    c0 += NSA_HEADS * HEAD_DIM
    y = proj(c0, 6 * LANES)
    kc = nr(group(y, 0), 3)
    ck_ref[0, 0] = kc[:, :HEAD_DIM]
    ck_ref[0, 1] = kc[:, HEAD_DIM:]
    vc = group(y, 1)
    cv_ref[0, 0] = vc[:, :HEAD_DIM]
    cv_ref[0, 1] = vc[:, HEAD_DIM:]
    split_groups(nr(group(y, 2), 4), sel_hot, ksa_ref, ksb_ref)
    split_groups(group(y, 3), 1.0, vsa_ref, vsb_ref)
    kw = nr(group(y, 4), 5)
    kw_rolled = pltpu.roll(kw, HEAD_DIM, 1)
    kw_ref[0, 0] = jnp.where(lo, kw, kw_rolled).astype(kw_ref.dtype)
    kw_ref[0, 1] = jnp.where(lo, kw_rolled, kw).astype(kw_ref.dtype)
    split_groups(group(y, 5), 1.0, vwa_ref, vwb_ref)
    c0 += 6 * LANES
    gt_ref[...] = _sigmoid(proj(c0, LANES))


def _sparse_proj(x2, an, sc, sh, w_pad, cos, sin, hgains, batch, seq):
    t, d = x2.shape
    rows = min(PROJ_ROWS, seq)
    per_b = seq // rows
    nb = seq // MOBA_BLOCK
    g = NSA_KV_HEADS
    w512 = MOBA_HEADS * HEAD_DIM
    row_spec = lambda w: pl.BlockSpec((rows, w), lambda i: (i, 0))
    mod_spec = pl.BlockSpec((1, 1, d), lambda i: (i // per_b, 0, 0))
    grp64 = pl.BlockSpec((1, g, rows, HEAD_DIM), lambda i: (i // per_b, 0, i % per_b, 0))
    grp128 = pl.BlockSpec((1, g, rows, LANES), lambda i: (i // per_b, 0, i % per_b, 0))
    tok = lambda dt: jax.ShapeDtypeStruct((t, w512), dt)
    kv = lambda dt, w: jax.ShapeDtypeStruct((batch, g, seq, w), dt)
    out_shape = [tok(F32), tok(BF16), tok(BF16), tok(BF16), tok(BF16),
                 jax.ShapeDtypeStruct((batch * nb, 1, w512), F32),
                 tok(F32),
                 kv(F32, HEAD_DIM), kv(F32, HEAD_DIM),
                 kv(BF16, LANES), kv(BF16, LANES), kv(BF16, LANES), kv(BF16, LANES),
                 kv(BF16, LANES), kv(BF16, LANES), kv(BF16, LANES),
                 jax.ShapeDtypeStruct((t, LANES), F32)]
    out_specs = ([row_spec(w512)] * 5
                 + [pl.BlockSpec((rows // MOBA_BLOCK, 1, w512), lambda i: (i, 0, 0)), row_spec(w512),
                    grp64, grp64] + [grp128] * 7 + [row_spec(LANES)])
    return pl.pallas_call(
        functools.partial(_sparse_proj_body, per_b=per_b),
        out_shape=out_shape,
        grid=(t // rows,),
        in_specs=[row_spec(d), _const_spec((1, d)), mod_spec, mod_spec,
                  _const_spec(w_pad.shape), row_spec(LANES), row_spec(LANES),
                  _const_spec(hgains.shape)],
        out_specs=out_specs,
        compiler_params=_cparams(1),
        name="sparse_in_proj",
    )(x2, an, sc, sh, w_pad, cos, sin, hgains)


def _moba_body(q_ref, ka_ref, kb_ref, va_ref, vb_ref, km_ref, o_ref, acc_a, acc_b):
    qi = pl.program_id(2)
    tq = q_ref.shape[1]
    kt = min(K_TILE, ka_ref.shape[1])
    nb = km_ref.shape[1]
    lo = _lane_iota(tq) < HEAD_DIM
    blk = lax.broadcasted_iota(jnp.int32, (nb, tq), 0)
    own = (qi * tq + lax.broadcasted_iota(jnp.int32, (nb, tq), 1)) // MOBA_BLOCK
    past = blk < own
    q = q_ref[0]
    km = km_ref[0]
    qp = []
    for half in (lo, ~lo):
        gate = lax.dot_general(km, jnp.where(half, q, 0.0), NT, precision=HI, preferred_element_type=F32)
        top = _topk_rows(jnp.where(past, gate, NEG_INF), min(MOBA_TOPK, nb)) & past
        bias = _lane_spread(jnp.where(top | (blk == own), 1.0, 0.0))
        qp.append(jnp.where(half, q * QK_SCALE, bias).astype(BF16))
    acc_a[...] = jnp.zeros_like(acc_a)
    acc_b[...] = jnp.zeros_like(acc_b)

    def tile(j, ms, width, mask):
        sa = lax.dot_general(qp[0], _row_tile(ka_ref, j, width), NT, preferred_element_type=F32)
        sb = lax.dot_general(qp[1], _row_tile(kb_ref, j, width), NT, preferred_element_type=F32)
        if mask is not None:
            sa = jnp.where(mask, sa, NEG_INF)
            sb = jnp.where(mask, sb, NEG_INF)
        ma, alpha, p = _softmax_step(ms[0], sa)
        _accumulate(acc_a, alpha, p, _row_tile(va_ref, j, width))
        mb, alpha, p = _softmax_step(ms[1], sb)
        _accumulate(acc_b, alpha, p, _row_tile(vb_ref, j, width))
        return ma, mb

    m0 = jnp.full((tq, 1), M_INIT, F32)
    _causal_sweep(qi, tq, kt, tile, (m0, m0))
    o_ref[0] = jnp.where(lo, _normalized(acc_a[...]), _normalized(acc_b[...])).astype(o_ref.dtype)


def _moba_attention(mq, mka, mkb, mva, mvb, km):
    batch, seq, w = mq.shape
    tq = min(Q_TILE, seq)
    nb = seq // MOBA_BLOCK
    q_spec = pl.BlockSpec((1, tq, LANES), lambda b, p, i: (b, i, p))
    kv_spec = pl.BlockSpec((1, seq, LANES), lambda b, p, i: (b, 0, p))
    return pl.pallas_call(
        _moba_body,
        out_shape=jax.ShapeDtypeStruct((batch, seq, w), BF16),
        grid=(batch, w // LANES, seq // tq),
        in_specs=[q_spec, kv_spec, kv_spec, kv_spec, kv_spec,
                  pl.BlockSpec((1, nb, LANES), lambda b, p, i: (b, 0, p))],
        out_specs=q_spec,
        scratch_shapes=[pltpu.VMEM((tq, LANES), F32)] * 2,
        compiler_params=_cparams(3),
        name="moba_attention",
    )(mq, mka, mkb, mva, mvb, km)


def _compress_body(t_ref, pos_ref, w1a_ref, w1b_ref, w2_ref, o_ref, *, transposed):
    t = t_ref[0]
    r = t.shape[0]
    a = jnp.dot(t + pos_ref[0:1, :], w1a_ref[...], precision=HI, preferred_element_type=F32)
    b = jnp.dot(t + pos_ref[1:2, :], w1b_ref[...], precision=HI, preferred_element_type=F32)
    hid = _silu(a + pltpu.roll(b, r - 1, 0))
    out = jnp.dot(hid, w2_ref[...], precision=HI, preferred_element_type=F32)
    rowi = lax.broadcasted_iota(jnp.int32, out.shape, 0)
    out = jnp.where(rowi < r - 1, out, 0.0)
    o_ref[0] = out.T if transposed else out


def _compress(t, pos, w1, w2, transposed):
    n, r, w = t.shape
    out_dims = (LANES, r) if transposed else (r, LANES)
    half = NSA_CMP_LEN // 2
    hidden = w1.shape[-1]
    w1a = w1[:half].reshape(w, hidden)
    w1b = w1[half:].reshape(w, hidden)
    w2d = jnp.concatenate([w2, w2], axis=1)
    return pl.pallas_call(
        functools.partial(_compress_body, transposed=transposed),
        out_shape=jax.ShapeDtypeStruct((n,) + out_dims, F32),
        grid=(n,),
        in_specs=[pl.BlockSpec((1, r, w), lambda i: (i, 0, 0)), _const_spec((2, w)),
                  _const_spec((w, hidden)), _const_spec((w, hidden)), _const_spec((hidden, LANES))],
        out_specs=pl.BlockSpec((1,) + out_dims, lambda i: (i, 0, 0)),
        compiler_params=_cparams(1),
        name="nsa_compress",
    )(t, pos.reshape(2, w), w1a, w1b, w2d)


def _nsa_cmp_body(q_ref, kc_ref, vct_ref, ovt_ref, oc_ref, bias_ref):
    qi = pl.program_id(2)
    tq = q_ref.shape[1]
    r = kc_ref.shape[1]
    nsel = ovt_ref.shape[0]
    kc = kc_ref[0]
    vct_b = vct_ref[0].astype(BF16)
    lo = _lane_iota(tq) < HEAD_DIM
    n = lax.broadcasted_iota(jnp.int32, (r, tq), 0)
    pos = qi * tq + lax.broadcasted_iota(jnp.int32, (r, tq), 1)
    ok = (n * NSA_CMP_STRIDE + (NSA_CMP_LEN - 1) <= pos) & (n < r - 1)
    psum = jnp.zeros((r, tq), F32)
    outs = []
    for hd in range(NSA_GROUP):
        qp = q_ref[0, :, (hd // 2) * LANES:(hd // 2 + 1) * LANES]
        qm = jnp.where(lo if hd % 2 == 0 else ~lo, qp, 0.0)
        s = lax.dot_general(kc, qm, NT, precision=HI, preferred_element_type=F32) * SCALE
        s = jnp.where(ok, s, NEG_INF)
        e = jnp.where(ok, jnp.exp(s - jnp.max(s, axis=0, keepdims=True)), 0.0)
        l = jnp.sum(e, axis=0, keepdims=True)
        p = e * jnp.where(l > 0.0, 1.0 / l, 0.0)
        psum = psum + p
        outs.append(jnp.dot(vct_b, p.astype(BF16), preferred_element_type=F32))
    even_rows = lax.broadcasted_iota(jnp.int32, (LANES, tq), 0) < HEAD_DIM
    for pr in range(NSA_GROUP // 2):
        oc_ref[0, :, pr * LANES:(pr + 1) * LANES] = jnp.where(even_rows, outs[2 * pr], outs[2 * pr + 1]).T
    imp = jnp.dot(ovt_ref[...], psum, precision=HI, preferred_element_type=F32)
    blk = lax.broadcasted_iota(jnp.int32, (nsel, tq), 0)
    cur = (qi * tq + lax.broadcasted_iota(jnp.int32, (nsel, tq), 1)) // NSA_SEL_BLOCK
    ok_blk = blk <= cur
    forced = ok_blk & ((blk == 0) | (blk >= cur - 1))
    score = jnp.where(ok_blk, jnp.where(forced, NSA_FORCE, imp), NEG_INF)
    chosen = _topk_rows(score, min(NSA_SEL_TOPK, nsel)) & ok_blk
    bias_ref[0, 0] = _lane_spread(jnp.where(chosen, 1.0, 0.0))


def _nsa_overlap_t(r, nsel):
    cs = np.arange(r) * NSA_CMP_STRIDE
    ss = np.arange(nsel) * NSA_SEL_BLOCK
    ov = (np.minimum(cs[None, :] + NSA_CMP_LEN, ss[:, None] + NSA_SEL_BLOCK)
          - np.maximum(cs[None, :], ss[:, None]))
    return jnp.asarray(np.clip(ov, 0, None) / NSA_CMP_LEN, dtype=F32)


def _nsa_cmp_attention(nq, kc, vct, batch, seq):
    g = NSA_KV_HEADS
    tq = min(CMP_TILE, seq)
    r = kc.shape[1]
    nsel = seq // NSA_SEL_BLOCK
    gw = NSA_GROUP * HEAD_DIM
    return pl.pallas_call(
        _nsa_cmp_body,
        out_shape=[jax.ShapeDtypeStruct((batch, seq, g * gw), F32),
                   jax.ShapeDtypeStruct((batch, g, seq, LANES), F32)],
        grid=(batch, g, seq // tq),
        in_specs=[pl.BlockSpec((1, tq, gw), lambda b, gi, i: (b, i, gi)),
                  pl.BlockSpec((1, r, LANES), lambda b, gi, i: (b * g + gi, 0, 0)),
                  pl.BlockSpec((1, LANES, r), lambda b, gi, i: (b * g + gi, 0, 0)),
                  _const_spec((nsel, r))],
        out_specs=[pl.BlockSpec((1, tq, gw), lambda b, gi, i: (b, i, gi)),
                   pl.BlockSpec((1, 1, tq, LANES), lambda b, gi, i: (b, gi, i, 0))],
        compiler_params=_cparams(3),
        name="nsa_compressed_attention",
    )(nq, kc, vct, _nsa_overlap_t(r, nsel))


def _nsa_main_body(q_ref, ksa_ref, ksb_ref, vsa_ref, vsb_ref, kw_ref, vwa_ref, vwb_ref,
                   bias_ref, gt_ref, oc_ref, o_ref, acc_ref):
    gi = pl.program_id(1)
    qi = pl.program_id(2)
    tq = q_ref.shape[1]
    kt = min(K_TILE, ksa_ref.shape[2])
    lane = _lane_iota(tq)
    lo = lane < HEAD_DIM
    bias = bias_ref[0, 0]
    q_sel, q_win = [], []
    for hd in range(NSA_GROUP):
        qs = q_ref[0, :, (hd // 2) * LANES:(hd // 2 + 1) * LANES] * QK_SCALE
        half = lo if hd % 2 == 0 else ~lo
        q_sel.append(jnp.where(half, qs, bias).astype(BF16))
        q_win.append(jnp.where(half, qs, 0.0).astype(BF16))
    k_sel = [r.at[0] for r in (ksa_ref, ksb_ref)]
    v_sel = [r.at[0] for r in (vsa_ref, vsb_ref)]
    v_win = [r.at[0] for r in (vwa_ref, vwb_ref)]

    def tile(j, ms, width, mask, qs, ks, vs):
        s = [lax.dot_general(qs[hd], _row_tile(ks[hd % 2], j, width), NT, preferred_element_type=F32)
             for hd in range(NSA_GROUP)]
        out = []
        for hd in range(NSA_GROUP):
            sh = s[hd] if mask is None else jnp.where(mask, s[hd], NEG_INF)
            m, alpha, p = _softmax_step(ms[hd], sh)
            _accumulate(acc_ref.at[hd], alpha, p, _row_tile(vs[hd % 2], j, width))
            out.append(m)
        return tuple(out)

    m0 = (jnp.full((tq, 1), M_INIT, F32),) * NSA_GROUP

    acc_ref[...] = jnp.zeros_like(acc_ref)
    _causal_sweep(qi, tq, kt, functools.partial(tile, qs=q_sel, ks=k_sel, vs=v_sel), m0)
    o_sel = [_normalized(acc_ref[hd]) for hd in range(NSA_GROUP)]

    acc_ref[...] = jnp.zeros_like(acc_ref)
    ms = m0
    k_win = [kw_ref.at[0]] * 2
    row = lax.broadcasted_iota(jnp.int32, (tq, tq), 0)
    col = lax.broadcasted_iota(jnp.int32, (tq, tq), 1)
    for d in range(NSA_WINDOW // tq + 1):
        delta = row - col + d * tq
        width = jnp.where(qi - d >= 0, NSA_WINDOW, 0)
        ms = tile(jnp.maximum(qi - d, 0), ms, tq, (delta >= 0) & (delta < width), q_win, k_win, v_win)
    o_win = [_normalized(acc_ref[hd]) for hd in range(NSA_GROUP)]

    gt = gt_ref[0]

    def gate(branch, hd):
        c = branch * NSA_HEADS + gi * NSA_GROUP + hd
        return jnp.sum(jnp.where(lane == c, gt, 0.0), axis=-1, keepdims=True)

    for pr in range(NSA_GROUP // 2):
        a, b = 2 * pr, 2 * pr + 1
        o = (jnp.where(lo, gate(0, a), gate(0, b)) * oc_ref[0, :, pr * LANES:(pr + 1) * LANES]
             + jnp.where(lo, gate(1, a) * o_sel[a], gate(1, b) * o_sel[b])
             + jnp.where(lo, gate(2, a) * o_win[a], gate(2, b) * o_win[b]))
        o_ref[0, :, pr * LANES:(pr + 1) * LANES] = o.astype(o_ref.dtype)


def _nsa_main_attention(nq, ksa, ksb, vsa, vsb, kw, vwa, vwb, bias, gates, oc):
    batch, seq, w = nq.shape
    g = NSA_KV_HEADS
    tq = min(Q_TILE, seq)
    gw = NSA_GROUP * HEAD_DIM
    kv_spec = pl.BlockSpec((1, 1, seq, LANES), lambda b, gi, i: (b, gi, 0, 0))
    q_spec = pl.BlockSpec((1, tq, gw), lambda b, gi, i: (b, i, gi))
    return pl.pallas_call(
        _nsa_main_body,
        out_shape=jax.ShapeDtypeStruct((batch, seq, w), BF16),
        grid=(batch, g, seq // tq),
        in_specs=[q_spec] + [kv_spec] * 7
                 + [pl.BlockSpec((1, 1, tq, LANES), lambda b, gi, i: (b, gi, i, 0)),
                    pl.BlockSpec((1, tq, LANES), lambda b, gi, i: (b, i, 0)),
                    q_spec],
        out_specs=q_spec,
        scratch_shapes=[pltpu.VMEM((NSA_GROUP, tq, LANES), F32)],
        compiler_params=_cparams(3),
        name="nsa_selected_window_attention",
    )(nq, ksa, ksb, vsa, vsb, kw, vwa, vwb, bias, gates, oc)


def _diff_proj_body(x_ref, an_ref, sc_ref, sh_ref, w_ref, cos_ref, sin_ref, hg_ref,
                    q_ref, k_ref, v_ref):
    h = _norm_mod(x_ref[...], an_ref[...], sc_ref[0], sh_ref[0]).astype(BF16)
    cos = cos_ref[...]
    sin = sin_ref[...]
    w = q_ref.shape[1]
    chunk = 4 * LANES
    for c0 in range(0, w, chunk):
        yq = jnp.dot(h, w_ref[:, c0:c0 + chunk], preferred_element_type=F32)
        yk = jnp.dot(h, w_ref[:, w + c0:w + c0 + chunk], preferred_element_type=F32)
        yv = jnp.dot(h, w_ref[:, 2 * w + c0:2 * w + c0 + chunk], preferred_element_type=F32)
        for p in range(chunk // LANES):
            sl = slice(p * LANES, (p + 1) * LANES)
            dst = slice(c0 + p * LANES, c0 + (p + 1) * LANES)
            q_ref[:, dst] = (_head_norm_rope(yq[:, sl], hg_ref[0:1, :], cos, sin) * QK_SCALE).astype(q_ref.dtype)
            k_ref[:, dst] = _head_norm_rope(yk[:, sl], hg_ref[1:2, :], cos, sin).astype(k_ref.dtype)
        v_ref[:, c0:c0 + chunk] = yv.astype(v_ref.dtype)


def _diff_proj(x2, an, sc, sh, w, cos, sin, hgains, seq):
    t, d = x2.shape
    rows = min(PROJ_ROWS, seq)
    per_b = seq // rows
    wq = w.shape[1] // 3
    row_spec = lambda n: pl.BlockSpec((rows, n), lambda i: (i, 0))
    mod_spec = pl.BlockSpec((1, 1, d), lambda i: (i // per_b, 0, 0))
    return pl.pallas_call(
        _diff_proj_body,
        out_shape=[jax.ShapeDtypeStruct((t, wq), BF16)] * 3,
        grid=(t // rows,),
        in_specs=[row_spec(d), _const_spec((1, d)), mod_spec, mod_spec, _const_spec(w.shape),
                  row_spec(LANES), row_spec(LANES), _const_spec(hgains.shape)],
        out_specs=[row_spec(wq)] * 3,
        compiler_params=_cparams(1),
        name="diff_in_proj",
    )(x2, an, sc, sh, w, cos, sin, hgains)


def _diff_body(lam_ref, on_ref, q_ref, k_ref, v_ref, o_ref, acc_ref, den_ref, *, lam_init):
    qi = pl.program_id(2)
    tq = q_ref.shape[1]
    kt = min(K_TILE, k_ref.shape[1])
    lo = _lane_iota(tq) < HEAD_DIM
    q = q_ref[0]
    zero = jnp.zeros_like(q)
    qb = (jnp.where(lo, q, zero), jnp.where(lo, zero, q))
    acc_ref[...] = jnp.zeros_like(acc_ref)
    den_ref[...] = jnp.zeros_like(den_ref)

    def tile(j, ms, width, mask):
        k = _row_tile(k_ref, j, width)
        v = _row_tile(v_ref, j, width)
        s = [lax.dot_general(qb[c], k, NT, preferred_element_type=F32) for c in range(2)]
        out = []
        for c in range(2):
            sc = s[c] if mask is None else jnp.where(mask, s[c], NEG_INF)
            m, alpha, p = _softmax_step(ms[c], sc)
            part = p[:, :LANES]
            for g in range(1, width // LANES):
                part = part + p[:, g * LANES:(g + 1) * LANES]
            den_ref[c] = alpha * den_ref[c] + part
            _accumulate(acc_ref.at[c], alpha, p, v)
            out.append(m)
        return tuple(out)

    m0 = jnp.full((tq, 1), M_INIT, F32)
    _causal_sweep(qi, tq, kt, tile, (m0, m0))
    lp = lam_ref[...]
    lam = (jnp.exp(jnp.sum(lp[0:1] * lp[1:2], axis=-1, keepdims=True))
           - jnp.exp(jnp.sum(lp[2:3] * lp[3:4], axis=-1, keepdims=True)) + lam_init)
    den = [jnp.sum(den_ref[c], axis=-1, keepdims=True) for c in range(2)]
    o = acc_ref[0] / den[0] - lam * (acc_ref[1] / den[1])
    ms = jnp.mean(o * o, axis=-1, keepdims=True)
    o = o * lax.rsqrt(ms + NORM_EPS) * on_ref[...] * (1.0 - lam_init)
    o_ref[0] = o.astype(o_ref.dtype)


def _diff_attention(q, k, v, lam_params, out_norm, lam_init):
    batch, seq, w = q.shape
    tq = min(Q_TILE, seq)
    kv_spec = pl.BlockSpec((1, seq, LANES), lambda b, h, i: (b, 0, h))
    q_spec = pl.BlockSpec((1, tq, LANES), lambda b, h, i: (b, i, h))
    return pl.pallas_call(
        functools.partial(_diff_body, lam_init=lam_init),
        out_shape=jax.ShapeDtypeStruct((batch, seq, w), BF16),
        grid=(batch, w // LANES, seq // tq),
        in_specs=[_const_spec(lam_params.shape), _const_spec((1, LANES)), q_spec, kv_spec, kv_spec],
        out_specs=q_spec,
        scratch_shapes=[pltpu.VMEM((2, tq, LANES), F32)] * 2,
        compiler_params=_cparams(3),
        name="diff_attention",
    )(lam_params, out_norm.reshape(1, LANES), q, k, v)


def _out_ffn_body(*refs, n_attn):
    x_ref = refs[0]
    attn = refs[1:1 + 2 * n_attn]
    ga_ref, fn_ref, sc_ref, sh_ref, gf_ref, wg_ref, wu_ref, wd_ref, o_ref = refs[1 + 2 * n_attn:]
    y = None
    for a in range(n_attn):
        part = jnp.dot(attn[2 * a][...], attn[2 * a + 1][...], preferred_element_type=F32)
        y = part if y is None else y + part
    x1 = x_ref[...] + ga_ref[0] * y
    h = _norm_mod(x1, fn_ref[...], sc_ref[0], sh_ref[0]).astype(BF16)
    acc = jnp.zeros(x1.shape, F32)
    for c0 in range(0, wg_ref.shape[1], FFN_CHUNK):
        gch = jnp.dot(h, wg_ref[:, c0:c0 + FFN_CHUNK], preferred_element_type=F32)
        uch = jnp.dot(h, wu_ref[:, c0:c0 + FFN_CHUNK], preferred_element_type=F32)
        act = (_silu(gch) * uch).astype(BF16)
        acc = acc + jnp.dot(act, wd_ref[c0:c0 + FFN_CHUNK, :], preferred_element_type=F32)
    o_ref[...] = x1 + gf_ref[0] * acc


def _out_ffn(x2, attn_pairs, ga, fn, sc, sh, gf, wg, wu, wd, seq):
    t, d = x2.shape
    rows = min(PROJ_ROWS, seq)
    per_b = seq // rows
    row_spec = lambda n: pl.BlockSpec((rows, n), lambda i: (i, 0))
    mod_spec = pl.BlockSpec((1, 1, d), lambda i: (i // per_b, 0, 0))
    resident = lambda a: pl.BlockSpec(a.shape, lambda i: (0, 0), pipeline_mode=pl.Buffered(1))
    args, specs = [x2], [row_spec(d)]
    for o, w in attn_pairs:
        args += [o, w]
        specs += [row_spec(o.shape[1]), resident(w)]
    args += [ga, fn, sc, sh, gf, wg, wu, wd]
    specs += [mod_spec, _const_spec((1, d)), mod_spec, mod_spec, mod_spec,
              resident(wg), resident(wu), resident(wd)]
    return pl.pallas_call(
        functools.partial(_out_ffn_body, n_attn=len(attn_pairs)),
        out_shape=jax.ShapeDtypeStruct((t, d), F32),
        grid=(t // rows,),
        in_specs=specs,
        out_specs=row_spec(d),
        compiler_params=_cparams(1),
        name="out_proj_swiglu",
    )(*args)


def _rope_lane_tables(positions):
    inv_freq = 1.0 / (ROPE_THETA ** (jnp.arange(0, ROT_DIM, 2, dtype=F32) / ROT_DIM))
    ang = positions.astype(F32).reshape(-1, 1) * inv_freq
    cos, sin = jnp.cos(ang), jnp.sin(ang)
    rest = HEAD_DIM - ROT_DIM
    ones = jnp.ones((ang.shape[0], rest), F32)
    zeros = jnp.zeros((ang.shape[0], rest), F32)
    cos_h = jnp.concatenate([cos, cos, ones], axis=1)
    sin_h = jnp.concatenate([-sin, sin, zeros], axis=1)
    return jnp.concatenate([cos_h, cos_h], axis=1), jnp.concatenate([sin_h, sin_h], axis=1)


def _pair_gain(gain):
    return jnp.concatenate([gain, gain]).reshape(1, LANES)


def _pad_rows(a, rows=8):
    return jnp.concatenate([a, jnp.zeros((rows - a.shape[0], a.shape[1]), a.dtype)], axis=0)


def kernel(x, c, positions, ada_w, ada_b, attn_norm, ffn_norm, ffn_w_gate, ffn_w_up, ffn_w_down, sp_w_in, sp_w_out, moba_q_norm, moba_k_norm, nsa_q_norm, nsa_k_norm, nsa_cmp_pos, nsa_cmp_w1, nsa_cmp_w2, diff_w_in, diff_w_out, diff_q_norm, diff_k_norm, diff_lambda, diff_out_norm):
    batch, seq, d = x.shape
    depth = ada_w.shape[0]
    assert seq % K_TILE == 0 and K_TILE % Q_TILE == 0 and Q_TILE == TAIL_TILE
    assert seq % MOBA_BLOCK == 0 and NSA_WINDOW % Q_TILE == 0
    assert (seq // MOBA_BLOCK) % SUBLANES == 0 and (seq // NSA_SEL_BLOCK) % SUBLANES == 0
    assert seq // MOBA_BLOCK <= HEAD_DIM and seq // NSA_SEL_BLOCK <= HEAD_DIM
    g = NSA_KV_HEADS
    cos, sin = _rope_lane_tables(positions)
    mod = _modulation(c, ada_w, ada_b).reshape(depth, batch, 6, 1, d)
    x2 = x.reshape(batch * seq, d)
    for i in range(depth):
        sh_a, sc_a, g_a, sh_f, sc_f, g_f = (mod[i, :, n] for n in range(6))
        an = attn_norm[i].reshape(1, d)
        j = i // 2
        to3 = lambda a: a.reshape(batch, seq, a.shape[-1])
        if i % 2 == 0:
            w_in = sp_w_in[j]
            pad = (-w_in.shape[1]) % LANES
            w_pad = jnp.concatenate([w_in, jnp.zeros((d, pad), w_in.dtype)], axis=1).astype(BF16)
            hgains = _pad_rows(jnp.concatenate(
                [_pair_gain(moba_q_norm[j]), _pair_gain(moba_k_norm[j]), _pair_gain(nsa_q_norm[j])]
                + [_pair_gain(nsa_k_norm[j, n]) for n in range(3)], axis=0))
            (mq, mka, mkb, mva, mvb, km, nq, ck, cv, ksa, ksb, vsa, vsb, kw, vwa, vwb, gates) = _sparse_proj(
                x2, an, sc_a, sh_a, w_pad, cos, sin, hgains, batch, seq)
            w512 = MOBA_HEADS * HEAD_DIM
            o_moba = _moba_attention(to3(mq), to3(mka), to3(mkb), to3(mva), to3(mvb),
                                     km.reshape(batch, seq // MOBA_BLOCK, w512))
            r = seq // NSA_CMP_STRIDE
            cw = NSA_CMP_STRIDE * HEAD_DIM
            kc = _compress(ck.reshape(batch * g, r, cw), nsa_cmp_pos[j, 0], nsa_cmp_w1[j, 0], nsa_cmp_w2[j, 0], False)
            vct = _compress(cv.reshape(batch * g, r, cw), nsa_cmp_pos[j, 1], nsa_cmp_w1[j, 1], nsa_cmp_w2[j, 1], True)
            oc, bias = _nsa_cmp_attention(to3(nq), kc, vct, batch, seq)
            o_nsa = _nsa_main_attention(to3(nq), ksa, ksb, vsa, vsb, kw, vwa, vwb, bias, to3(gates), oc)
            w_out = sp_w_out[j].astype(BF16)
            attn_pairs = [(o_moba.reshape(batch * seq, w512), w_out[:w512]),
                          (o_nsa.reshape(batch * seq, w512), w_out[w512:])]
        else:
            lam_init = 0.8 - 0.6 * float(np.exp(-0.3 * i))
            hgains = _pad_rows(jnp.concatenate(
                [_pair_gain(diff_q_norm[j]), _pair_gain(diff_k_norm[j])], axis=0))
            q, k, v = _diff_proj(x2, an, sc_a, sh_a, diff_w_in[j].astype(BF16), cos, sin, hgains, seq)
            o = _diff_attention(to3(q), to3(k), to3(v), diff_lambda[j], diff_out_norm[j], lam_init)
            attn_pairs = [(o.reshape(batch * seq, o.shape[-1]), diff_w_out[j].astype(BF16))]
        x2 = _out_ffn(x2, attn_pairs, g_a, ffn_norm[i].reshape(1, d), sc_f, sh_f, g_f,
                      ffn_w_gate[i].astype(BF16), ffn_w_up[i].astype(BF16),
                      ffn_w_down[i].astype(BF16), seq)
    return x2.reshape(batch, seq, d)
```

```python
import functools

import numpy as np
import jax
import jax.numpy as jnp
from jax import lax
from jax.experimental import pallas as pl
from jax.experimental.pallas import tpu as pltpu

F32 = jnp.float32
BF16 = jnp.bfloat16
HI = lax.Precision.HIGHEST

LANES = 128
HEAD_DIM = 64
ROT_DIM = HEAD_DIM // 4
ROT_HALF = ROT_DIM // 2
ROPE_THETA = 500000.0
NORM_EPS = 1e-6
NEG_INF = -1e30
M_INIT = -1e20
SCALE = HEAD_DIM ** -0.5
QK_SCALE = SCALE * 1.4426950408889634

MOBA_HEADS = 8
MOBA_BLOCK = 256
MOBA_TOPK = 3

NSA_HEADS = 8
NSA_KV_HEADS = 2
NSA_GROUP = NSA_HEADS // NSA_KV_HEADS
NSA_CMP_LEN = 32
NSA_CMP_STRIDE = 16
NSA_SEL_BLOCK = 64
NSA_SEL_TOPK = 16
NSA_WINDOW = 512
NSA_FORCE = 1e4

DIFF_HEADS = 8
DIFF_HEADS_PER_STEP = 2
MOBA_PAIRS_PER_STEP = 2

Q_TILE = 512
K_TILE = 1024
TAIL_TILE = 512
CMP_TILE = 256
SUBLANES = 8
PROJ_ROWS = 512
FFN_CHUNK = 256
VMEM_LIMIT = 56 * 2 ** 20

NT = (((1,), (1,)), ((), ()))


def _cparams(n_axes):
    return pltpu.CompilerParams(dimension_semantics=("arbitrary",) * n_axes,
                                vmem_limit_bytes=VMEM_LIMIT)


def _const_spec(shape):
    return pl.BlockSpec(shape, lambda *_: (0,) * len(shape))


def _silu(t):
    return t / (1.0 + jnp.exp(-t))


def _sigmoid(t):
    return 1.0 / (1.0 + jnp.exp(-t))


def _lane_iota(rows):
    return lax.broadcasted_iota(jnp.int32, (rows, LANES), 1)


def _mod_body(c_ref, w_ref, b_ref, o_ref):
    a = _silu(c_ref[...])
    o_ref[0] = jnp.dot(a, w_ref[0], precision=HI, preferred_element_type=F32) + b_ref[0]


def _modulation(c, ada_w, ada_b):
    depth, d, n = ada_w.shape
    b = c.shape[0]
    tn = n // 4
    return pl.pallas_call(
        _mod_body,
        out_shape=jax.ShapeDtypeStruct((depth, b, n), F32),
        grid=(depth, n // tn),
        in_specs=[pl.BlockSpec((b, d), lambda i, j: (0, 0)),
                  pl.BlockSpec((1, d, tn), lambda i, j: (i, 0, j)),
                  pl.BlockSpec((1, 1, tn), lambda i, j: (i, 0, j))],
        out_specs=pl.BlockSpec((1, b, tn), lambda i, j: (i, 0, j)),
        compiler_params=_cparams(2),
        name="adaln_modulation",
    )(c, ada_w, ada_b.reshape(depth, 1, n))


def _norm_mod(x, gain, sc, sh):
    ms = jnp.mean(x * x, axis=-1, keepdims=True)
    return x * lax.rsqrt(ms + NORM_EPS) * gain * (1.0 + sc) + sh


def _head_sumsq(y):
    wide = 2 * LANES
    same_head = (lax.broadcasted_iota(jnp.int32, (wide, wide), 0) // HEAD_DIM
                 == lax.broadcasted_iota(jnp.int32, (wide, wide), 1) // HEAD_DIM)
    ones_bd = jnp.where(same_head, 1.0, 0.0).astype(BF16)
    y2 = (y * y).astype(BF16)
    out = [jnp.dot(y2[:, c:c + wide], ones_bd, preferred_element_type=F32)
           for c in range(0, y.shape[1], wide)]
    return out[0] if len(out) == 1 else jnp.concatenate(out, axis=1)


def _head_norm_rope(y, ss, gain, cos, sin):
    rows = y.shape[0]
    lane = _lane_iota(rows)
    yn = y * lax.rsqrt(ss * (1.0 / HEAD_DIM) + NORM_EPS) * gain
    ahead = pltpu.roll(yn, LANES - ROT_HALF, 1)
    behind = pltpu.roll(yn, ROT_HALF, 1)
    partner = jnp.where((lane & (HEAD_DIM - 1)) < ROT_HALF, ahead, behind)
    return yn * cos + partner * sin


def _split_bf16(a, terms):
    out = []
    for _ in range(terms):
        t = a.astype(BF16)
        out.append(t)
        a = a - t.astype(F32)
    return out


def _dot_nt_3pass(a_terms, b):
    a_hi, a_lo = a_terms
    b_hi, b_lo = _split_bf16(b, 2)
    d = lambda u, v: lax.dot_general(u, v, NT, preferred_element_type=F32)
    return d(a_hi, b_hi) + (d(a_hi, b_lo) + d(a_lo, b_hi))


def _softmax_step(m, s):
    m_new = jnp.maximum(m, jnp.max(s, axis=-1, keepdims=True))
    return m_new, jnp.exp2(m - m_new), jnp.exp2(s - m_new)


def _accumulate(acc_ref, alpha, p, v):
    acc_ref[...] = alpha * acc_ref[...] + jnp.dot(p.astype(BF16), v, preferred_element_type=F32)


def _normalized(acc):
    return acc / pltpu.roll(acc, HEAD_DIM, 1)


def _topk_rows(score, k):
    n, cols = score.shape
    nblk = n // SUBLANES
    blocks = [score[b * SUBLANES:(b + 1) * SUBLANES] for b in range(nblk)]
    sub = lax.broadcasted_iota(jnp.int32, (SUBLANES, cols), 0)
    ranks = [jnp.zeros((SUBLANES, cols), jnp.int32) for _ in range(nblk)]
    for i in range(n):
        si = score[i:i + 1, :]
        for b in range(nblk):
            if b > i // SUBLANES:
                beats = si >= blocks[b]
            elif b < i // SUBLANES:
                beats = si > blocks[b]
            else:
                beats = (si > blocks[b]) | ((si == blocks[b]) & (sub > i % SUBLANES))
            ranks[b] = ranks[b] + jnp.where(beats, 1, 0)
    return jnp.concatenate(ranks, axis=0) < k


def _lane_spread(allowed):
    n, cols = allowed.shape
    parts = [allowed, allowed]
    if n < HEAD_DIM:
        fill = jnp.zeros((HEAD_DIM - n, cols), F32)
        parts = [allowed, fill, allowed, fill]
    return jnp.where(jnp.concatenate(parts, axis=0).T > 0.5, 0.0, NEG_INF)


def _row_tile(ref, j, rows):
    return ref[0, pl.ds(pl.multiple_of(j * rows, rows), rows), :]


def _causal_sweep(qi, tq, kt, tile, state):
    n_wide = (qi * tq) // kt
    state = lax.fori_loop(0, n_wide, lambda j, c: tile(j, c, kt, None), state)
    row = lax.broadcasted_iota(jnp.int32, (tq, tq), 0)
    col = lax.broadcasted_iota(jnp.int32, (tq, tq), 1)
    return lax.fori_loop(n_wide * (kt // tq), qi + 1,
                         lambda j, c: tile(j, c, tq, col - row <= (qi - j) * tq), state)


def _sparse_proj_body(x_ref, an_ref, sc_ref, sh_ref, w_ref, cos_ref, sin_ref, hg_ref,
                      mq_ref, mka_ref, mkb_ref, mva_ref, mvb_ref, km_ref, nq_ref, ck_ref, cv_ref,
                      ksa_ref, ksb_ref, vsa_ref, vsb_ref, kw_ref, vwa_ref, vwb_ref, gt_ref,
                      *, per_b):
    rows = x_ref.shape[0]
    h = _norm_mod(x_ref[...], an_ref[...], sc_ref[0], sh_ref[0]).astype(BF16)
    cos = cos_ref[...]
    sin = sin_ref[...]
    lane = _lane_iota(rows)
    lo = lane < HEAD_DIM
    seq_pos = (pl.program_id(0) % per_b) * rows + lax.broadcasted_iota(jnp.int32, (rows, LANES), 0)
    lane_blk = lane & (HEAD_DIM - 1)
    moba_hot = jnp.where(lane_blk == seq_pos // MOBA_BLOCK, 1.0, 0.0)
    sel_hot = jnp.where(lane_blk == seq_pos // NSA_SEL_BLOCK, 1.0, 0.0)

    def proj(c0, n):
        return jnp.dot(h, w_ref[:, c0:c0 + n], preferred_element_type=F32)

    def group(y, p):
        return y[:, p * LANES:(p + 1) * LANES]

    def nr(y, ss, p, gi):
        return _head_norm_rope(group(y, p), group(ss, p), hg_ref[gi:gi + 1, :], cos, sin)

    def split_groups(y, fill, a_ref, b_ref):
        rolled = pltpu.roll(y, HEAD_DIM, 1)
        a_ref[0, 0] = jnp.where(lo, y, fill).astype(a_ref.dtype)
        b_ref[0, 0] = jnp.where(lo, fill, rolled).astype(b_ref.dtype)
        a_ref[0, 1] = jnp.where(lo, rolled, fill).astype(a_ref.dtype)
        b_ref[0, 1] = jnp.where(lo, fill, y).astype(b_ref.dtype)

    moba_w = MOBA_HEADS * HEAD_DIM
    pairs = moba_w // LANES
    y = proj(0, moba_w)
    ss = _head_sumsq(y)
    for p in range(pairs):
        mq_ref[:, p * LANES:(p + 1) * LANES] = nr(y, ss, p, 0)
    y = proj(moba_w, moba_w)
    ss = _head_sumsq(y)
    for p in range(pairs):
        sl = slice(p * LANES, (p + 1) * LANES)
        k = nr(y, ss, p, 1)
        mka_ref[:, sl] = jnp.where(lo, k, moba_hot).astype(mka_ref.dtype)
        mkb_ref[:, sl] = jnp.where(lo, moba_hot, k).astype(mkb_ref.dtype)
        for r in range(rows // MOBA_BLOCK):
            km_ref[r, :, sl] = jnp.mean(k[r * MOBA_BLOCK:(r + 1) * MOBA_BLOCK], axis=0, keepdims=True)
    y = proj(2 * moba_w, moba_w)
    for p in range(pairs):
        sl = slice(p * LANES, (p + 1) * LANES)
        mva_ref[:, sl] = jnp.where(lo, group(y, p), 1.0).astype(mva_ref.dtype)
        mvb_ref[:, sl] = jnp.where(lo, 1.0, group(y, p)).astype(mvb_ref.dtype)
    c0 = 3 * moba_w
    y = proj(c0, NSA_HEADS * HEAD_DIM)
    ss = _head_sumsq(y)
    for p in range(NSA_HEADS * HEAD_DIM // LANES):
        nq_ref[:, p * LANES:(p + 1) * LANES] = nr(y, ss, p, 2)
    c0 += NSA_HEADS * HEAD_DIM
    y = proj(c0, 6 * LANES)
    ss = _head_sumsq(y)
    kc = nr(y, ss, 0, 3)
    ck_ref[0, 0] = kc[:, :HEAD_DIM]
    ck_ref[0, 1] = kc[:, HEAD_DIM:]
    vc = group(y, 1)
    cv_ref[0, 0] = vc[:, :HEAD_DIM]
    cv_ref[0, 1] = vc[:, HEAD_DIM:]
    split_groups(nr(y, ss, 2, 4), sel_hot, ksa_ref, ksb_ref)
    split_groups(group(y, 3), 1.0, vsa_ref, vsb_ref)
    kw = nr(y, ss, 4, 5)
    kw_rolled = pltpu.roll(kw, HEAD_DIM, 1)
    kw_ref[0, 0] = jnp.where(lo, kw, kw_rolled).astype(kw_ref.dtype)
    kw_ref[0, 1] = jnp.where(lo, kw_rolled, kw).astype(kw_ref.dtype)
    split_groups(group(y, 5), 1.0, vwa_ref, vwb_ref)
    c0 += 6 * LANES
    gt_ref[...] = _sigmoid(proj(c0, LANES))


def _sparse_proj(x2, an, sc, sh, w_pad, cos, sin, hgains, batch, seq):
    t, d = x2.shape
    rows = min(PROJ_ROWS, seq)
    per_b = seq // rows
    nb = seq // MOBA_BLOCK
    g = NSA_KV_HEADS
    w512 = MOBA_HEADS * HEAD_DIM
    row_spec = lambda w: pl.BlockSpec((rows, w), lambda i: (i, 0))
    mod_spec = pl.BlockSpec((1, 1, d), lambda i: (i // per_b, 0, 0))
    grp64 = pl.BlockSpec((1, g, rows, HEAD_DIM), lambda i: (i // per_b, 0, i % per_b, 0))
    grp128 = pl.BlockSpec((1, g, rows, LANES), lambda i: (i // per_b, 0, i % per_b, 0))
    tok = lambda dt: jax.ShapeDtypeStruct((t, w512), dt)
    kv = lambda dt, w: jax.ShapeDtypeStruct((batch, g, seq, w), dt)
    out_shape = [tok(F32), tok(BF16), tok(BF16), tok(BF16), tok(BF16),
                 jax.ShapeDtypeStruct((batch * nb, 1, w512), F32),
                 tok(F32),
                 kv(F32, HEAD_DIM), kv(F32, HEAD_DIM),
                 kv(BF16, LANES), kv(BF16, LANES), kv(BF16, LANES), kv(BF16, LANES),
                 kv(BF16, LANES), kv(BF16, LANES), kv(BF16, LANES),
                 jax.ShapeDtypeStruct((t, LANES), F32)]
    out_specs = ([row_spec(w512)] * 5
                 + [pl.BlockSpec((rows // MOBA_BLOCK, 1, w512), lambda i: (i, 0, 0)), row_spec(w512),
                    grp64, grp64] + [grp128] * 7 + [row_spec(LANES)])
    return pl.pallas_call(
        functools.partial(_sparse_proj_body, per_b=per_b),
        out_shape=out_shape,
        grid=(t // rows,),
        in_specs=[row_spec(d), _const_spec((1, d)), mod_spec, mod_spec,
                  _const_spec(w_pad.shape), row_spec(LANES), row_spec(LANES),
                  _const_spec(hgains.shape)],
        out_specs=out_specs,
        compiler_params=_cparams(1),
        name="sparse_in_proj",
    )(x2, an, sc, sh, w_pad, cos, sin, hgains)


def _moba_body(q_ref, ka_ref, kb_ref, va_ref, vb_ref, km_ref, o_ref, acc_ref):
    qi = pl.program_id(2)
    tq = q_ref.shape[1]
    kt = min(K_TILE, ka_ref.shape[1])
    nb = km_ref.shape[1]
    pairs = q_ref.shape[2] // LANES
    lo = _lane_iota(tq) < HEAD_DIM
    blk = lax.broadcasted_iota(jnp.int32, (nb, tq), 0)
    own = (qi * tq + lax.broadcasted_iota(jnp.int32, (nb, tq), 1)) // MOBA_BLOCK
    past = blk < own
    qp = []
    for pr in range(pairs):
        sl = slice(pr * LANES, (pr + 1) * LANES)
        q = q_ref[0, :, sl]
        km = km_ref[0, :, sl]
        for half in (lo, ~lo):
            gate = lax.dot_general(km, jnp.where(half, q, 0.0), NT, precision=HI, preferred_element_type=F32)
            top = _topk_rows(jnp.where(past, gate, NEG_INF), min(MOBA_TOPK, nb)) & past
            bias = _lane_spread(jnp.where(top | (blk == own), 1.0, 0.0))
            qp.append(jnp.where(half, q * QK_SCALE, bias).astype(BF16))
    acc_ref[...] = jnp.zeros_like(acc_ref)
    k_refs = (ka_ref, kb_ref)
    v_refs = (va_ref, vb_ref)

    def tile(j, ms, width, mask):
        ks = [_row_tile(r, j, width) for r in k_refs]
        vs = [_row_tile(r, j, width) for r in v_refs]
        lanes = lambda a, c: a[:, (c // 2) * LANES:(c // 2 + 1) * LANES]
        s = [lax.dot_general(qp[c], lanes(ks[c % 2], c), NT, preferred_element_type=F32)
             for c in range(2 * pairs)]
        out = []
        for c in range(2 * pairs):
            sc = s[c] if mask is None else jnp.where(mask, s[c], NEG_INF)
            m, alpha, p = _softmax_step(ms[c], sc)
            _accumulate(acc_ref.at[c], alpha, p, lanes(vs[c % 2], c))
            out.append(m)
        return tuple(out)

    m0 = jnp.full((tq, 1), M_INIT, F32)
    _causal_sweep(qi, tq, kt, tile, (m0,) * (2 * pairs))
    for pr in range(pairs):
        o = jnp.where(lo, _normalized(acc_ref[2 * pr]), _normalized(acc_ref[2 * pr + 1]))
        o_ref[0, :, pr * LANES:(pr + 1) * LANES] = o.astype(o_ref.dtype)


def _moba_attention(mq, mka, mkb, mva, mvb, km):
    batch, seq, w = mq.shape
    tq = min(Q_TILE, seq)
    nb = seq // MOBA_BLOCK
    gw = MOBA_PAIRS_PER_STEP * LANES
    q_spec = pl.BlockSpec((1, tq, gw), lambda b, p, i: (b, i, p))
    kv_spec = pl.BlockSpec((1, seq, gw), lambda b, p, i: (b, 0, p))
    return pl.pallas_call(
        _moba_body,
        out_shape=jax.ShapeDtypeStruct((batch, seq, w), BF16),
        grid=(batch, w // gw, seq // tq),
        in_specs=[q_spec, kv_spec, kv_spec, kv_spec, kv_spec,
                  pl.BlockSpec((1, nb, gw), lambda b, p, i: (b, 0, p))],
        out_specs=q_spec,
        scratch_shapes=[pltpu.VMEM((2 * MOBA_PAIRS_PER_STEP, tq, LANES), F32)],
        compiler_params=_cparams(3),
        name="moba_attention",
    )(mq, mka, mkb, mva, mvb, km)


def _compress_body(t_ref, pos_ref, w1a_ref, w1b_ref, w2_ref, o_ref, *, transposed):
    t = t_ref[0]
    r = t.shape[0]
    a = jnp.dot(t + pos_ref[0:1, :], w1a_ref[...], precision=HI, preferred_element_type=F32)
    b = jnp.dot(t + pos_ref[1:2, :], w1b_ref[...], precision=HI, preferred_element_type=F32)
    hid = _silu(a + pltpu.roll(b, r - 1, 0))
    out = jnp.dot(hid, w2_ref[...], precision=HI, preferred_element_type=F32)
    rowi = lax.broadcasted_iota(jnp.int32, out.shape, 0)
    out = jnp.where(rowi < r - 1, out, 0.0)
    o_ref[0] = out.T if transposed else out


def _compress(t, pos, w1, w2, transposed):
    n, r, w = t.shape
    out_dims = (LANES, r) if transposed else (r, LANES)
    half = NSA_CMP_LEN // 2
    hidden = w1.shape[-1]
    w1a = w1[:half].reshape(w, hidden)
    w1b = w1[half:].reshape(w, hidden)
    w2d = jnp.concatenate([w2, w2], axis=1)
    return pl.pallas_call(
        functools.partial(_compress_body, transposed=transposed),
        out_shape=jax.ShapeDtypeStruct((n,) + out_dims, F32),
        grid=(n,),
        in_specs=[pl.BlockSpec((1, r, w), lambda i: (i, 0, 0)), _const_spec((2, w)),
                  _const_spec((w, hidden)), _const_spec((w, hidden)), _const_spec((hidden, LANES))],
        out_specs=pl.BlockSpec((1,) + out_dims, lambda i: (i, 0, 0)),
        compiler_params=_cparams(1),
        name="nsa_compress",
    )(t, pos.reshape(2, w), w1a, w1b, w2d)


def _nsa_cmp_body(q_ref, kc_ref, vct_ref, ovt_ref, oc_ref, bias_ref):
    qi = pl.program_id(2)
    tq = q_ref.shape[1]
    r = kc_ref.shape[1]
    nsel = ovt_ref.shape[0]
    kc_terms = _split_bf16(kc_ref[0], 2)
    vct_b = vct_ref[0].astype(BF16)
    lo = _lane_iota(tq) < HEAD_DIM
    n = lax.broadcasted_iota(jnp.int32, (r, tq), 0)
    pos = qi * tq + lax.broadcasted_iota(jnp.int32, (r, tq), 1)
    ok = (n * NSA_CMP_STRIDE + (NSA_CMP_LEN - 1) <= pos) & (n < r - 1)
    psum = jnp.zeros((r, tq), F32)
    outs = []
    for hd in range(NSA_GROUP):
        qp = q_ref[0, :, (hd // 2) * LANES:(hd // 2 + 1) * LANES]
        qm = jnp.where(lo if hd % 2 == 0 else ~lo, qp, 0.0)
        s = _dot_nt_3pass(kc_terms, qm) * SCALE
        s = jnp.where(ok, s, NEG_INF)
        e = jnp.where(ok, jnp.exp(s - jnp.max(s, axis=0, keepdims=True)), 0.0)
        l = jnp.sum(e, axis=0, keepdims=True)
        p = e * jnp.where(l > 0.0, 1.0 / l, 0.0)
        psum = psum + p
        outs.append(jnp.dot(vct_b, p.astype(BF16), preferred_element_type=F32))
    even_rows = lax.broadcasted_iota(jnp.int32, (LANES, tq), 0) < HEAD_DIM
    for pr in range(NSA_GROUP // 2):
        oc_ref[0, :, pr * LANES:(pr + 1) * LANES] = jnp.where(even_rows, outs[2 * pr], outs[2 * pr + 1]).T
    ovt = ovt_ref[...].astype(BF16)
    imp = sum(jnp.dot(ovt, t, preferred_element_type=F32) for t in _split_bf16(psum, 3))
    blk = lax.broadcasted_iota(jnp.int32, (nsel, tq), 0)
    cur = (qi * tq + lax.broadcasted_iota(jnp.int32, (nsel, tq), 1)) // NSA_SEL_BLOCK
    ok_blk = blk <= cur
    forced = ok_blk & ((blk == 0) | (blk >= cur - 1))
    score = jnp.where(ok_blk, jnp.where(forced, NSA_FORCE, imp), NEG_INF)
    chosen = _topk_rows(score, min(NSA_SEL_TOPK, nsel)) & ok_blk
    bias_ref[0, 0] = _lane_spread(jnp.where(chosen, 1.0, 0.0))


def _nsa_overlap_t(r, nsel):
    cs = np.arange(r) * NSA_CMP_STRIDE
    ss = np.arange(nsel) * NSA_SEL_BLOCK
    ov = (np.minimum(cs[None, :] + NSA_CMP_LEN, ss[:, None] + NSA_SEL_BLOCK)
          - np.maximum(cs[None, :], ss[:, None]))
    return jnp.asarray(np.clip(ov, 0, None) / NSA_CMP_LEN, dtype=F32)


def _nsa_cmp_attention(nq, kc, vct, batch, seq):
    g = NSA_KV_HEADS
    tq = min(CMP_TILE, seq)
    r = kc.shape[1]
    nsel = seq // NSA_SEL_BLOCK
    gw = NSA_GROUP * HEAD_DIM
    return pl.pallas_call(
        _nsa_cmp_body,
        out_shape=[jax.ShapeDtypeStruct((batch, seq, g * gw), F32),
                   jax.ShapeDtypeStruct((batch, g, seq, LANES), F32)],
        grid=(batch, g, seq // tq),
        in_specs=[pl.BlockSpec((1, tq, gw), lambda b, gi, i: (b, i, gi)),
                  pl.BlockSpec((1, r, LANES), lambda b, gi, i: (b * g + gi, 0, 0)),
                  pl.BlockSpec((1, LANES, r), lambda b, gi, i: (b * g + gi, 0, 0)),
                  _const_spec((nsel, r))],
        out_specs=[pl.BlockSpec((1, tq, gw), lambda b, gi, i: (b, i, gi)),
                   pl.BlockSpec((1, 1, tq, LANES), lambda b, gi, i: (b, gi, i, 0))],
        compiler_params=_cparams(3),
        name="nsa_compressed_attention",
    )(nq, kc, vct, _nsa_overlap_t(r, nsel))


def _nsa_main_body(q_ref, ksa_ref, ksb_ref, vsa_ref, vsb_ref, kw_ref, vwa_ref, vwb_ref,
                   bias_ref, gt_ref, oc_ref, o_ref, acc_ref):
    gi = pl.program_id(1)
    qi = pl.program_id(2)
    tq = q_ref.shape[1]
    kt = min(K_TILE, ksa_ref.shape[2])
    lane = _lane_iota(tq)
    lo = lane < HEAD_DIM
    bias = bias_ref[0, 0]
    q_sel, q_win = [], []
    for hd in range(NSA_GROUP):
        qs = q_ref[0, :, (hd // 2) * LANES:(hd // 2 + 1) * LANES] * QK_SCALE
        half = lo if hd % 2 == 0 else ~lo
        q_sel.append(jnp.where(half, qs, bias).astype(BF16))
        q_win.append(jnp.where(half, qs, 0.0).astype(BF16))
    k_sel = [r.at[0] for r in (ksa_ref, ksb_ref)]
    v_sel = [r.at[0] for r in (vsa_ref, vsb_ref)]
    v_win = [r.at[0] for r in (vwa_ref, vwb_ref)]

    def tile(j, ms, width, mask, qs, ks, vs, align=None):
        rows = pl.ds(pl.multiple_of(j * (align or width), align or width), width)
        s = [lax.dot_general(qs[hd], ks[hd % 2][0, rows, :], NT, preferred_element_type=F32)
             for hd in range(NSA_GROUP)]
        out = []
        for hd in range(NSA_GROUP):
            sh = s[hd] if mask is None else jnp.where(mask, s[hd], NEG_INF)
            m, alpha, p = _softmax_step(ms[hd], sh)
            _accumulate(acc_ref.at[hd], alpha, p, vs[hd % 2][0, rows, :])
            out.append(m)
        return tuple(out)

    m0 = (jnp.full((tq, 1), M_INIT, F32),) * NSA_GROUP

    acc_ref[...] = jnp.zeros_like(acc_ref)
    _causal_sweep(qi, tq, kt, functools.partial(tile, qs=q_sel, ks=k_sel, vs=v_sel), m0)
    o_sel = [_normalized(acc_ref[hd]) for hd in range(NSA_GROUP)]

    acc_ref[...] = jnp.zeros_like(acc_ref)
    k_win = [kw_ref.at[0]] * 2
    span = NSA_WINDOW + tq
    first = jnp.maximum(qi * tq - NSA_WINDOW, 0)
    delta = (qi * tq - first + lax.broadcasted_iota(jnp.int32, (tq, span), 0)
             - lax.broadcasted_iota(jnp.int32, (tq, span), 1))
    tile(first // tq, m0, span, (delta >= 0) & (delta < NSA_WINDOW), q_win, k_win, v_win, align=tq)
    o_win = [_normalized(acc_ref[hd]) for hd in range(NSA_GROUP)]

    gt = gt_ref[0]

    def gate(branch, hd):
        c = branch * NSA_HEADS + gi * NSA_GROUP + hd
        return jnp.sum(jnp.where(lane == c, gt, 0.0), axis=-1, keepdims=True)

    for pr in range(NSA_GROUP // 2):
        a, b = 2 * pr, 2 * pr + 1
        o = (jnp.where(lo, gate(0, a), gate(0, b)) * oc_ref[0, :, pr * LANES:(pr + 1) * LANES]
             + jnp.where(lo, gate(1, a) * o_sel[a], gate(1, b) * o_sel[b])
             + jnp.where(lo, gate(2, a) * o_win[a], gate(2, b) * o_win[b]))
        o_ref[0, :, pr * LANES:(pr + 1) * LANES] = o.astype(o_ref.dtype)


def _nsa_main_attention(nq, ksa, ksb, vsa, vsb, kw, vwa, vwb, bias, gates, oc):
    batch, seq, w = nq.shape
    g = NSA_KV_HEADS
    tq = min(Q_TILE, seq)
    gw = NSA_GROUP * HEAD_DIM
    kv_spec = pl.BlockSpec((1, 1, seq, LANES), lambda b, gi, i: (b, gi, 0, 0))
    q_spec = pl.BlockSpec((1, tq, gw), lambda b, gi, i: (b, i, gi))
    return pl.pallas_call(
        _nsa_main_body,
        out_shape=jax.ShapeDtypeStruct((batch, seq, w), BF16),
        grid=(batch, g, seq // tq),
        in_specs=[q_spec] + [kv_spec] * 7
                 + [pl.BlockSpec((1, 1, tq, LANES), lambda b, gi, i: (b, gi, i, 0)),
                    pl.BlockSpec((1, tq, LANES), lambda b, gi, i: (b, i, 0)),
                    q_spec],
        out_specs=q_spec,
        scratch_shapes=[pltpu.VMEM((NSA_GROUP, tq, LANES), F32)],
        compiler_params=_cparams(3),
        name="nsa_selected_window_attention",
    )(nq, ksa, ksb, vsa, vsb, kw, vwa, vwb, bias, gates, oc)


def _diff_proj_body(x_ref, an_ref, sc_ref, sh_ref, w_ref, cos_ref, sin_ref, hg_ref,
                    q_ref, k_ref, v_ref):
    h = _norm_mod(x_ref[...], an_ref[...], sc_ref[0], sh_ref[0]).astype(BF16)
    cos = cos_ref[...]
    sin = sin_ref[...]
    w = q_ref.shape[1]
    chunk = 4 * LANES
    for c0 in range(0, w, chunk):
        yq = jnp.dot(h, w_ref[:, c0:c0 + chunk], preferred_element_type=F32)
        yk = jnp.dot(h, w_ref[:, w + c0:w + c0 + chunk], preferred_element_type=F32)
        yv = jnp.dot(h, w_ref[:, 2 * w + c0:2 * w + c0 + chunk], preferred_element_type=F32)
        sq, sk = _head_sumsq(yq), _head_sumsq(yk)
        for p in range(chunk // LANES):
            sl = slice(p * LANES, (p + 1) * LANES)
            dst = slice(c0 + p * LANES, c0 + (p + 1) * LANES)
            q_ref[:, dst] = (_head_norm_rope(yq[:, sl], sq[:, sl], hg_ref[0:1, :], cos, sin) * QK_SCALE).astype(q_ref.dtype)
            k_ref[:, dst] = _head_norm_rope(yk[:, sl], sk[:, sl], hg_ref[1:2, :], cos, sin).astype(k_ref.dtype)
        v_ref[:, c0:c0 + chunk] = yv.astype(v_ref.dtype)


def _diff_proj(x2, an, sc, sh, w, cos, sin, hgains, seq):
    t, d = x2.shape
    rows = min(PROJ_ROWS, seq)
    per_b = seq // rows
    wq = w.shape[1] // 3
    row_spec = lambda n: pl.BlockSpec((rows, n), lambda i: (i, 0))
    mod_spec = pl.BlockSpec((1, 1, d), lambda i: (i // per_b, 0, 0))
    return pl.pallas_call(
        _diff_proj_body,
        out_shape=[jax.ShapeDtypeStruct((t, wq), BF16)] * 3,
        grid=(t // rows,),
        in_specs=[row_spec(d), _const_spec((1, d)), mod_spec, mod_spec, _const_spec(w.shape),
                  row_spec(LANES), row_spec(LANES), _const_spec(hgains.shape)],
        out_specs=[row_spec(wq)] * 3,
        compiler_params=_cparams(1),
        name="diff_in_proj",
    )(x2, an, sc, sh, w, cos, sin, hgains)


def _diff_body(lam_ref, on_ref, q_ref, k_ref, v_ref, o_ref, acc_ref, den_ref, *, lam_init):
    qi = pl.program_id(2)
    tq = q_ref.shape[1]
    kt = min(K_TILE, k_ref.shape[1])
    heads = q_ref.shape[2] // LANES
    lo = _lane_iota(tq) < HEAD_DIM
    qb = []
    for hd in range(heads):
        q = q_ref[0, :, hd * LANES:(hd + 1) * LANES]
        zero = jnp.zeros_like(q)
        qb += [jnp.where(lo, q, zero), jnp.where(lo, zero, q)]
    acc_ref[...] = jnp.zeros_like(acc_ref)
    den_ref[...] = jnp.zeros_like(den_ref)

    def tile(j, ms, width, mask):
        k = _row_tile(k_ref, j, width)
        v = _row_tile(v_ref, j, width)
        s = [lax.dot_general(qb[c], k[:, (c // 2) * LANES:(c // 2 + 1) * LANES], NT,
                             preferred_element_type=F32) for c in range(2 * heads)]
        out = []
        for c in range(2 * heads):
            sc = s[c] if mask is None else jnp.where(mask, s[c], NEG_INF)
            m, alpha, p = _softmax_step(ms[c], sc)
            part = p[:, :LANES]
            for g in range(1, width // LANES):
                part = part + p[:, g * LANES:(g + 1) * LANES]
            den_ref[c] = alpha * den_ref[c] + part
            _accumulate(acc_ref.at[c], alpha, p, v[:, (c // 2) * LANES:(c // 2 + 1) * LANES])
            out.append(m)
        return tuple(out)

    m0 = jnp.full((tq, 1), M_INIT, F32)
    _causal_sweep(qi, tq, kt, tile, (m0,) * (2 * heads))
    lp = lam_ref[...]
    lam = (jnp.exp(jnp.sum(lp[0:1] * lp[1:2], axis=-1, keepdims=True))
           - jnp.exp(jnp.sum(lp[2:3] * lp[3:4], axis=-1, keepdims=True)) + lam_init)
    for hd in range(heads):
        den = [jnp.sum(den_ref[2 * hd + c], axis=-1, keepdims=True) for c in range(2)]
        o = acc_ref[2 * hd] / den[0] - lam * (acc_ref[2 * hd + 1] / den[1])
        ms = jnp.mean(o * o, axis=-1, keepdims=True)
        o = o * lax.rsqrt(ms + NORM_EPS) * on_ref[...] * (1.0 - lam_init)
        o_ref[0, :, hd * LANES:(hd + 1) * LANES] = o.astype(o_ref.dtype)


def _diff_attention(q, k, v, lam_params, out_norm, lam_init):
    batch, seq, w = q.shape
    tq = min(Q_TILE, seq)
    gw = DIFF_HEADS_PER_STEP * LANES
    kv_spec = pl.BlockSpec((1, seq, gw), lambda b, h, i: (b, 0, h))
    q_spec = pl.BlockSpec((1, tq, gw), lambda b, h, i: (b, i, h))
    return pl.pallas_call(
        functools.partial(_diff_body, lam_init=lam_init),
        out_shape=jax.ShapeDtypeStruct((batch, seq, w), BF16),
        grid=(batch, w // gw, seq // tq),
        in_specs=[_const_spec(lam_params.shape), _const_spec((1, LANES)), q_spec, kv_spec, kv_spec],
        out_specs=q_spec,
        scratch_shapes=[pltpu.VMEM((2 * DIFF_HEADS_PER_STEP, tq, LANES), F32)] * 2,
        compiler_params=_cparams(3),
        name="diff_attention",
    )(lam_params, out_norm.reshape(1, LANES), q, k, v)


def _out_ffn_body(*refs, n_attn):
    x_ref = refs[0]
    attn = refs[1:1 + 2 * n_attn]
    ga_ref, fn_ref, sc_ref, sh_ref, gf_ref, wg_ref, wu_ref, wd_ref, o_ref = refs[1 + 2 * n_attn:]
    y = None
    for a in range(n_attn):
        part = jnp.dot(attn[2 * a][...], attn[2 * a + 1][...], preferred_element_type=F32)
        y = part if y is None else y + part
    x1 = x_ref[...] + ga_ref[0] * y
    h = _norm_mod(x1, fn_ref[...], sc_ref[0], sh_ref[0]).astype(BF16)
    acc = jnp.zeros(x1.shape, F32)
    for c0 in range(0, wg_ref.shape[1], FFN_CHUNK):
        gch = jnp.dot(h, wg_ref[:, c0:c0 + FFN_CHUNK], preferred_element_type=F32)
        uch = jnp.dot(h, wu_ref[:, c0:c0 + FFN_CHUNK], preferred_element_type=F32)
        act = (_silu(gch) * uch).astype(BF16)
        acc = acc + jnp.dot(act, wd_ref[c0:c0 + FFN_CHUNK, :], preferred_element_type=F32)
    o_ref[...] = x1 + gf_ref[0] * acc


def _out_ffn(x2, attn_pairs, ga, fn, sc, sh, gf, wg, wu, wd, seq):
    t, d = x2.shape
    rows = min(PROJ_ROWS, seq)
    per_b = seq // rows
    row_spec = lambda n: pl.BlockSpec((rows, n), lambda i: (i, 0))
    mod_spec = pl.BlockSpec((1, 1, d), lambda i: (i // per_b, 0, 0))
    resident = lambda a: pl.BlockSpec(a.shape, lambda i: (0, 0), pipeline_mode=pl.Buffered(1))
    args, specs = [x2], [row_spec(d)]
    for o, w in attn_pairs:
        args += [o, w]
        specs += [row_spec(o.shape[1]), resident(w)]
    args += [ga, fn, sc, sh, gf, wg, wu, wd]
    specs += [mod_spec, _const_spec((1, d)), mod_spec, mod_spec, mod_spec,
              resident(wg), resident(wu), resident(wd)]
    return pl.pallas_call(
        functools.partial(_out_ffn_body, n_attn=len(attn_pairs)),
        out_shape=jax.ShapeDtypeStruct((t, d), F32),
        grid=(t // rows,),
        in_specs=specs,
        out_specs=row_spec(d),
        compiler_params=_cparams(1),
        name="out_proj_swiglu",
    )(*args)


def _rope_lane_tables(positions):
    inv_freq = 1.0 / (ROPE_THETA ** (jnp.arange(0, ROT_DIM, 2, dtype=F32) / ROT_DIM))
    lane = np.arange(LANES) % HEAD_DIM
    rotary = lane < ROT_DIM
    freq = jnp.where(rotary, inv_freq[lane % ROT_HALF], 0.0)
    sign = jnp.asarray(np.where(rotary, np.where(lane < ROT_HALF, -1.0, 1.0), 0.0), F32)
    ang = positions.astype(F32).reshape(-1, 1) * freq
    return jnp.where(rotary, jnp.cos(ang), 1.0), jnp.sin(ang) * sign


def _pair_gain(gain):
    return jnp.concatenate([gain, gain]).reshape(1, LANES)


def _pad_rows(a, rows=8):
    return jnp.concatenate([a, jnp.zeros((rows - a.shape[0], a.shape[1]), a.dtype)], axis=0)


def kernel(x, c, positions, ada_w, ada_b, attn_norm, ffn_norm, ffn_w_gate, ffn_w_up, ffn_w_down, sp_w_in, sp_w_out, moba_q_norm, moba_k_norm, nsa_q_norm, nsa_k_norm, nsa_cmp_pos, nsa_cmp_w1, nsa_cmp_w2, diff_w_in, diff_w_out, diff_q_norm, diff_k_norm, diff_lambda, diff_out_norm):
    batch, seq, d = x.shape
    depth = ada_w.shape[0]
    assert seq % K_TILE == 0 and K_TILE % Q_TILE == 0 and Q_TILE == TAIL_TILE
    assert seq % MOBA_BLOCK == 0 and NSA_WINDOW % Q_TILE == 0 and NSA_WINDOW + Q_TILE <= seq
    assert (seq // MOBA_BLOCK) % SUBLANES == 0 and (seq // NSA_SEL_BLOCK) % SUBLANES == 0
    assert seq // MOBA_BLOCK <= HEAD_DIM and seq // NSA_SEL_BLOCK <= HEAD_DIM
    g = NSA_KV_HEADS
    cos, sin = _rope_lane_tables(positions)
    mod = _modulation(c, ada_w, ada_b).reshape(depth, batch, 6, 1, d)
    x2 = x.reshape(batch * seq, d)
    for i in range(depth):
        sh_a, sc_a, g_a, sh_f, sc_f, g_f = (mod[i, :, n] for n in range(6))
        an = attn_norm[i].reshape(1, d)
        j = i // 2
        to3 = lambda a: a.reshape(batch, seq, a.shape[-1])
        if i % 2 == 0:
            w_in = sp_w_in[j]
            pad = (-w_in.shape[1]) % LANES
            w_pad = jnp.concatenate([w_in, jnp.zeros((d, pad), w_in.dtype)], axis=1).astype(BF16)
            hgains = _pad_rows(jnp.concatenate(
                [_pair_gain(moba_q_norm[j]), _pair_gain(moba_k_norm[j]), _pair_gain(nsa_q_norm[j])]
                + [_pair_gain(nsa_k_norm[j, n]) for n in range(3)], axis=0))
            (mq, mka, mkb, mva, mvb, km, nq, ck, cv, ksa, ksb, vsa, vsb, kw, vwa, vwb, gates) = _sparse_proj(
                x2, an, sc_a, sh_a, w_pad, cos, sin, hgains, batch, seq)
            w512 = MOBA_HEADS * HEAD_DIM
            o_moba = _moba_attention(to3(mq), to3(mka), to3(mkb), to3(mva), to3(mvb),
                                     km.reshape(batch, seq // MOBA_BLOCK, w512))
            r = seq // NSA_CMP_STRIDE
            cw = NSA_CMP_STRIDE * HEAD_DIM
            kc = _compress(ck.reshape(batch * g, r, cw), nsa_cmp_pos[j, 0], nsa_cmp_w1[j, 0], nsa_cmp_w2[j, 0], False)
            vct = _compress(cv.reshape(batch * g, r, cw), nsa_cmp_pos[j, 1], nsa_cmp_w1[j, 1], nsa_cmp_w2[j, 1], True)
            oc, bias = _nsa_cmp_attention(to3(nq), kc, vct, batch, seq)
            o_nsa = _nsa_main_attention(to3(nq), ksa, ksb, vsa, vsb, kw, vwa, vwb, bias, to3(gates), oc)
            w_out = sp_w_out[j].astype(BF16)
            attn_pairs = [(o_moba.reshape(batch * seq, w512), w_out[:w512]),
                          (o_nsa.reshape(batch * seq, w512), w_out[w512:])]
        else:
            lam_init = 0.8 - 0.6 * float(np.exp(-0.3 * i))
            hgains = _pad_rows(jnp.concatenate(
                [_pair_gain(diff_q_norm[j]), _pair_gain(diff_k_norm[j])], axis=0))
            q, k, v = _diff_proj(x2, an, sc_a, sh_a, diff_w_in[j].astype(BF16), cos, sin, hgains, seq)
            o = _diff_attention(to3(q), to3(k), to3(v), diff_lambda[j], diff_out_norm[j], lam_init)
            attn_pairs = [(o.reshape(batch * seq, o.shape[-1]), diff_w_out[j].astype(BF16))]
        x2 = _out_ffn(x2, attn_pairs, g_a, ffn_norm[i].reshape(1, d), sc_f, sh_f, g_f,
                      ffn_w_gate[i].astype(BF16), ffn_w_up[i].astype(BF16),
                      ffn_w_down[i].astype(BF16), seq)
    return x2.reshape(batch, seq, d)
```

```python
import functools

import numpy as np
import jax
import jax.numpy as jnp
from jax import lax
from jax.experimental import pallas as pl
from jax.experimental.pallas import tpu as pltpu

F32 = jnp.float32
BF16 = jnp.bfloat16
HI = lax.Precision.HIGHEST

LANES = 128
HEAD_DIM = 64
ROT_DIM = HEAD_DIM // 4
ROT_HALF = ROT_DIM // 2
ROPE_THETA = 500000.0
NORM_EPS = 1e-6
NEG_INF = -1e30
M_INIT = -1e20
SCALE = HEAD_DIM ** -0.5
QK_SCALE = SCALE * 1.4426950408889634

MOBA_HEADS = 8
MOBA_BLOCK = 256
MOBA_TOPK = 3

NSA_HEADS = 8
NSA_KV_HEADS = 2
NSA_GROUP = NSA_HEADS // NSA_KV_HEADS
NSA_CMP_LEN = 32
NSA_CMP_STRIDE = 16
NSA_SEL_BLOCK = 64
NSA_SEL_TOPK = 16
NSA_WINDOW = 512
NSA_FORCE = 1e4

DIFF_HEADS = 8
DIFF_HEADS_PER_STEP = 2
MOBA_PAIRS_PER_STEP = 2

Q_TILE = 512
K_TILE = 1024
TAIL_TILE = 512
CMP_TILE = 256
SUBLANES = 8
PROJ_ROWS = 512
FFN_CHUNK = 256
VMEM_LIMIT = 56 * 2 ** 20

NT = (((1,), (1,)), ((), ()))


def _cparams(n_axes):
    return pltpu.CompilerParams(dimension_semantics=("arbitrary",) * n_axes,
                                vmem_limit_bytes=VMEM_LIMIT)


def _const_spec(shape):
    return pl.BlockSpec(shape, lambda *_: (0,) * len(shape))


def _silu(t):
    return t / (1.0 + jnp.exp(-t))


def _sigmoid(t):
    return 1.0 / (1.0 + jnp.exp(-t))


def _lane_iota(rows):
    return lax.broadcasted_iota(jnp.int32, (rows, LANES), 1)


def _mod_body(c_ref, w_ref, b_ref, o_ref):
    a = _silu(c_ref[...])
    o_ref[0] = jnp.dot(a, w_ref[0], precision=HI, preferred_element_type=F32) + b_ref[0]


def _modulation(c, ada_w, ada_b):
    depth, d, n = ada_w.shape
    b = c.shape[0]
    tn = n // 4
    return pl.pallas_call(
        _mod_body,
        out_shape=jax.ShapeDtypeStruct((depth, b, n), F32),
        grid=(depth, n // tn),
        in_specs=[pl.BlockSpec((b, d), lambda i, j: (0, 0)),
                  pl.BlockSpec((1, d, tn), lambda i, j: (i, 0, j)),
                  pl.BlockSpec((1, 1, tn), lambda i, j: (i, 0, j))],
        out_specs=pl.BlockSpec((1, b, tn), lambda i, j: (i, 0, j)),
        compiler_params=_cparams(2),
        name="adaln_modulation",
    )(c, ada_w, ada_b.reshape(depth, 1, n))


def _norm_mod(x, gain, sc, sh):
    ms = jnp.mean(x * x, axis=-1, keepdims=True)
    return x * lax.rsqrt(ms + NORM_EPS) * gain * (1.0 + sc) + sh


def _head_sumsq(y):
    wide = 2 * LANES
    same_head = (lax.broadcasted_iota(jnp.int32, (wide, wide), 0) // HEAD_DIM
                 == lax.broadcasted_iota(jnp.int32, (wide, wide), 1) // HEAD_DIM)
    ones_bd = jnp.where(same_head, 1.0, 0.0).astype(BF16)
    y2 = (y * y).astype(BF16)
    out = [jnp.dot(y2[:, c:c + wide], ones_bd, preferred_element_type=F32)
           for c in range(0, y.shape[1], wide)]
    return out[0] if len(out) == 1 else jnp.concatenate(out, axis=1)


def _head_norm_rope(y, ss, gain, cos, sin):
    rows = y.shape[0]
    lane = _lane_iota(rows)
    yn = y * lax.rsqrt(ss * (1.0 / HEAD_DIM) + NORM_EPS) * gain
    ahead = pltpu.roll(yn, LANES - ROT_HALF, 1)
    behind = pltpu.roll(yn, ROT_HALF, 1)
    partner = jnp.where((lane & (HEAD_DIM - 1)) < ROT_HALF, ahead, behind)
    return yn * cos + partner * sin


def _split_bf16(a, terms):
    out = []
    for _ in range(terms):
        t = a.astype(BF16)
        out.append(t)
        a = a - t.astype(F32)
    return out


def _dot_nt_3pass(a_terms, b):
    a_hi, a_lo = a_terms
    b_hi, b_lo = _split_bf16(b, 2)
    d = lambda u, v: lax.dot_general(u, v, NT, preferred_element_type=F32)
    return d(a_hi, b_hi) + (d(a_hi, b_lo) + d(a_lo, b_hi))


def _softmax_step(m, s, bias):
    sb = s.astype(BF16)
    if bias is not None:
        sb = sb + bias
    m_new = jnp.maximum(m, jnp.max(sb, axis=-1, keepdims=True).astype(F32))
    return m_new, jnp.exp2(m - m_new), jnp.exp2(sb - m_new.astype(BF16))


def _mask_bias(mask):
    return jnp.where(mask, 0.0, NEG_INF).astype(BF16)


def _accumulate(acc_ref, alpha, p, v):
    acc_ref[...] = alpha * acc_ref[...] + jnp.dot(p, v, preferred_element_type=F32)


def _normalized(acc):
    return acc / pltpu.roll(acc, HEAD_DIM, 1)


def _topk_rows(score, k):
    n, cols = score.shape
    nblk = n // SUBLANES
    blocks = [score[b * SUBLANES:(b + 1) * SUBLANES] for b in range(nblk)]
    sub = lax.broadcasted_iota(jnp.int32, (SUBLANES, cols), 0)
    ranks = [jnp.zeros((SUBLANES, cols), jnp.int32) for _ in range(nblk)]
    for i in range(n):
        si = score[i:i + 1, :]
        for b in range(nblk):
            if b > i // SUBLANES:
                beats = si >= blocks[b]
            elif b < i // SUBLANES:
                beats = si > blocks[b]
            else:
                beats = (si > blocks[b]) | ((si == blocks[b]) & (sub > i % SUBLANES))
            ranks[b] = ranks[b] + jnp.where(beats, 1, 0)
    return jnp.concatenate(ranks, axis=0) < k


def _lane_spread(allowed):
    n, cols = allowed.shape
    parts = [allowed, allowed]
    if n < HEAD_DIM:
        fill = jnp.zeros((HEAD_DIM - n, cols), F32)
        parts = [allowed, fill, allowed, fill]
    return jnp.where(jnp.concatenate(parts, axis=0).T > 0.5, 0.0, NEG_INF)


def _row_tile(ref, j, rows):
    return ref[0, pl.ds(pl.multiple_of(j * rows, rows), rows), :]


def _causal_sweep(qi, tq, kt, tile, state):
    n_wide = (qi * tq) // kt
    state = lax.fori_loop(0, n_wide, lambda j, c: tile(j, c, kt, None), state)
    state = lax.fori_loop(n_wide * (kt // tq), qi, lambda j, c: tile(j, c, tq, None), state)
    row = lax.broadcasted_iota(jnp.int32, (tq, tq), 0)
    col = lax.broadcasted_iota(jnp.int32, (tq, tq), 1)
    return tile(qi, state, tq, _mask_bias(col <= row))


def _sparse_proj_body(x_ref, an_ref, sc_ref, sh_ref, w_ref, cos_ref, sin_ref, hg_ref,
                      mq_ref, mka_ref, mkb_ref, mva_ref, mvb_ref, km_ref, nq_ref, ck_ref, cv_ref,
                      ksa_ref, ksb_ref, vsa_ref, vsb_ref, kw_ref, vwa_ref, vwb_ref, gt_ref,
                      *, per_b):
    rows = x_ref.shape[0]
    h = _norm_mod(x_ref[...], an_ref[...], sc_ref[0], sh_ref[0]).astype(BF16)
    cos = cos_ref[...]
    sin = sin_ref[...]
    lane = _lane_iota(rows)
    lo = lane < HEAD_DIM
    seq_pos = (pl.program_id(0) % per_b) * rows + lax.broadcasted_iota(jnp.int32, (rows, LANES), 0)
    lane_blk = lane & (HEAD_DIM - 1)
    moba_hot = jnp.where(lane_blk == seq_pos // MOBA_BLOCK, 1.0, 0.0)
    sel_hot = jnp.where(lane_blk == seq_pos // NSA_SEL_BLOCK, 1.0, 0.0)

    def proj(c0, n):
        return jnp.dot(h, w_ref[:, c0:c0 + n], preferred_element_type=F32)

    def group(y, p):
        return y[:, p * LANES:(p + 1) * LANES]

    def nr(y, ss, p, gi):
        return _head_norm_rope(group(y, p), group(ss, p), hg_ref[gi:gi + 1, :], cos, sin)

    def split_groups(y, fill, a_ref, b_ref):
        rolled = pltpu.roll(y, HEAD_DIM, 1)
        a_ref[0, 0] = jnp.where(lo, y, fill).astype(a_ref.dtype)
        b_ref[0, 0] = jnp.where(lo, fill, rolled).astype(b_ref.dtype)
        a_ref[0, 1] = jnp.where(lo, rolled, fill).astype(a_ref.dtype)
        b_ref[0, 1] = jnp.where(lo, fill, y).astype(b_ref.dtype)

    moba_w = MOBA_HEADS * HEAD_DIM
    pairs = moba_w // LANES
    y = proj(0, moba_w)
    ss = _head_sumsq(y)
    for p in range(pairs):
        mq_ref[:, p * LANES:(p + 1) * LANES] = nr(y, ss, p, 0)
    y = proj(moba_w, moba_w)
    ss = _head_sumsq(y)
    for p in range(pairs):
        sl = slice(p * LANES, (p + 1) * LANES)
        k = nr(y, ss, p, 1)
        mka_ref[:, sl] = jnp.where(lo, k, moba_hot).astype(mka_ref.dtype)
        mkb_ref[:, sl] = jnp.where(lo, moba_hot, k).astype(mkb_ref.dtype)
        for r in range(rows // MOBA_BLOCK):
            km_ref[r, :, sl] = jnp.mean(k[r * MOBA_BLOCK:(r + 1) * MOBA_BLOCK], axis=0, keepdims=True)
    y = proj(2 * moba_w, moba_w)
    for p in range(pairs):
        sl = slice(p * LANES, (p + 1) * LANES)
        mva_ref[:, sl] = jnp.where(lo, group(y, p), 1.0).astype(mva_ref.dtype)
        mvb_ref[:, sl] = jnp.where(lo, 1.0, group(y, p)).astype(mvb_ref.dtype)
    c0 = 3 * moba_w
    y = proj(c0, NSA_HEADS * HEAD_DIM)
    ss = _head_sumsq(y)
    for p in range(NSA_HEADS * HEAD_DIM // LANES):
        nq_ref[:, p * LANES:(p + 1) * LANES] = nr(y, ss, p, 2)
    c0 += NSA_HEADS * HEAD_DIM
    y = proj(c0, 6 * LANES)
    ss = _head_sumsq(y)
    kc = nr(y, ss, 0, 3)
    ck_ref[0, 0] = kc[:, :HEAD_DIM]
    ck_ref[0, 1] = kc[:, HEAD_DIM:]
    vc = group(y, 1)
    cv_ref[0, 0] = vc[:, :HEAD_DIM]
    cv_ref[0, 1] = vc[:, HEAD_DIM:]
    split_groups(nr(y, ss, 2, 4), sel_hot, ksa_ref, ksb_ref)
    split_groups(group(y, 3), 1.0, vsa_ref, vsb_ref)
    kw = nr(y, ss, 4, 5)
    kw_rolled = pltpu.roll(kw, HEAD_DIM, 1)
    kw_ref[0, 0] = jnp.where(lo, kw, kw_rolled).astype(kw_ref.dtype)
    kw_ref[0, 1] = jnp.where(lo, kw_rolled, kw).astype(kw_ref.dtype)
    split_groups(group(y, 5), 1.0, vwa_ref, vwb_ref)
    c0 += 6 * LANES
    gt_ref[...] = _sigmoid(proj(c0, LANES))


def _sparse_proj(x2, an, sc, sh, w_pad, cos, sin, hgains, batch, seq):
    t, d = x2.shape
    rows = min(PROJ_ROWS, seq)
    per_b = seq // rows
    nb = seq // MOBA_BLOCK
    g = NSA_KV_HEADS
    w512 = MOBA_HEADS * HEAD_DIM
    row_spec = lambda w: pl.BlockSpec((rows, w), lambda i: (i, 0))
    mod_spec = pl.BlockSpec((1, 1, d), lambda i: (i // per_b, 0, 0))
    grp64 = pl.BlockSpec((1, g, rows, HEAD_DIM), lambda i: (i // per_b, 0, i % per_b, 0))
    grp128 = pl.BlockSpec((1, g, rows, LANES), lambda i: (i // per_b, 0, i % per_b, 0))
    tok = lambda dt: jax.ShapeDtypeStruct((t, w512), dt)
    kv = lambda dt, w: jax.ShapeDtypeStruct((batch, g, seq, w), dt)
    out_shape = [tok(F32), tok(BF16), tok(BF16), tok(BF16), tok(BF16),
                 jax.ShapeDtypeStruct((batch * nb, 1, w512), F32),
                 tok(F32),
                 kv(F32, HEAD_DIM), kv(F32, HEAD_DIM),
                 kv(BF16, LANES), kv(BF16, LANES), kv(BF16, LANES), kv(BF16, LANES),
                 kv(BF16, LANES), kv(BF16, LANES), kv(BF16, LANES),
                 jax.ShapeDtypeStruct((t, LANES), F32)]
    out_specs = ([row_spec(w512)] * 5
                 + [pl.BlockSpec((rows // MOBA_BLOCK, 1, w512), lambda i: (i, 0, 0)), row_spec(w512),
                    grp64, grp64] + [grp128] * 7 + [row_spec(LANES)])
    return pl.pallas_call(
        functools.partial(_sparse_proj_body, per_b=per_b),
        out_shape=out_shape,
        grid=(t // rows,),
        in_specs=[row_spec(d), _const_spec((1, d)), mod_spec, mod_spec,
                  _const_spec(w_pad.shape), row_spec(LANES), row_spec(LANES),
                  _const_spec(hgains.shape)],
        out_specs=out_specs,
        compiler_params=_cparams(1),
        name="sparse_in_proj",
    )(x2, an, sc, sh, w_pad, cos, sin, hgains)


def _moba_body(q_ref, ka_ref, kb_ref, va_ref, vb_ref, km_ref, o_ref, acc_ref):
    qi = pl.program_id(2)
    tq = q_ref.shape[1]
    kt = min(K_TILE, ka_ref.shape[1])
    nb = km_ref.shape[1]
    pairs = q_ref.shape[2] // LANES
    lo = _lane_iota(tq) < HEAD_DIM
    blk = lax.broadcasted_iota(jnp.int32, (nb, tq), 0)
    own = (qi * tq + lax.broadcasted_iota(jnp.int32, (nb, tq), 1)) // MOBA_BLOCK
    past = blk < own
    qp = []
    for pr in range(pairs):
        sl = slice(pr * LANES, (pr + 1) * LANES)
        q = q_ref[0, :, sl]
        km = km_ref[0, :, sl]
        for half in (lo, ~lo):
            gate = lax.dot_general(km, jnp.where(half, q, 0.0), NT, precision=HI, preferred_element_type=F32)
            top = _topk_rows(jnp.where(past, gate, NEG_INF), min(MOBA_TOPK, nb)) & past
            bias = _lane_spread(jnp.where(top | (blk == own), 1.0, 0.0))
            qp.append(jnp.where(half, q * QK_SCALE, bias).astype(BF16))
    acc_ref[...] = jnp.zeros_like(acc_ref)
    k_refs = (ka_ref, kb_ref)
    v_refs = (va_ref, vb_ref)

    def tile(j, ms, width, mask):
        ks = [_row_tile(r, j, width) for r in k_refs]
        vs = [_row_tile(r, j, width) for r in v_refs]
        lanes = lambda a, c: a[:, (c // 2) * LANES:(c // 2 + 1) * LANES]
        s = [lax.dot_general(qp[c], lanes(ks[c % 2], c), NT, preferred_element_type=F32)
             for c in range(2 * pairs)]
        out = []
        for c in range(2 * pairs):
            m, alpha, p = _softmax_step(ms[c], s[c], mask)
            _accumulate(acc_ref.at[c], alpha, p, lanes(vs[c % 2], c))
            out.append(m)
        return tuple(out)

    m0 = jnp.full((tq, 1), M_INIT, F32)
    _causal_sweep(qi, tq, kt, tile, (m0,) * (2 * pairs))
    for pr in range(pairs):
        o = jnp.where(lo, _normalized(acc_ref[2 * pr]), _normalized(acc_ref[2 * pr + 1]))
        o_ref[0, :, pr * LANES:(pr + 1) * LANES] = o.astype(o_ref.dtype)


def _moba_attention(mq, mka, mkb, mva, mvb, km):
    batch, seq, w = mq.shape
    tq = min(Q_TILE, seq)
    nb = seq // MOBA_BLOCK
    gw = MOBA_PAIRS_PER_STEP * LANES
    q_spec = pl.BlockSpec((1, tq, gw), lambda b, p, i: (b, i, p))
    kv_spec = pl.BlockSpec((1, seq, gw), lambda b, p, i: (b, 0, p))
    return pl.pallas_call(
        _moba_body,
        out_shape=jax.ShapeDtypeStruct((batch, seq, w), BF16),
        grid=(batch, w // gw, seq // tq),
        in_specs=[q_spec, kv_spec, kv_spec, kv_spec, kv_spec,
                  pl.BlockSpec((1, nb, gw), lambda b, p, i: (b, 0, p))],
        out_specs=q_spec,
        scratch_shapes=[pltpu.VMEM((2 * MOBA_PAIRS_PER_STEP, tq, LANES), F32)],
        compiler_params=_cparams(3),
        name="moba_attention",
    )(mq, mka, mkb, mva, mvb, km)


def _compress_body(t_ref, pos_ref, w1a_ref, w1b_ref, w2_ref, o_ref, *, transposed):
    t = t_ref[0]
    r = t.shape[0]
    a = jnp.dot(t + pos_ref[0:1, :], w1a_ref[...], precision=HI, preferred_element_type=F32)
    b = jnp.dot(t + pos_ref[1:2, :], w1b_ref[...], precision=HI, preferred_element_type=F32)
    hid = _silu(a + pltpu.roll(b, r - 1, 0))
    out = jnp.dot(hid, w2_ref[...], precision=HI, preferred_element_type=F32)
    rowi = lax.broadcasted_iota(jnp.int32, out.shape, 0)
    out = jnp.where(rowi < r - 1, out, 0.0)
    o_ref[0] = out.T if transposed else out


def _compress(t, pos, w1, w2, transposed):
    n, r, w = t.shape
    out_dims = (LANES, r) if transposed else (r, LANES)
    half = NSA_CMP_LEN // 2
    hidden = w1.shape[-1]
    w1a = w1[:half].reshape(w, hidden)
    w1b = w1[half:].reshape(w, hidden)
    w2d = jnp.concatenate([w2, w2], axis=1)
    return pl.pallas_call(
        functools.partial(_compress_body, transposed=transposed),
        out_shape=jax.ShapeDtypeStruct((n,) + out_dims, F32),
        grid=(n,),
        in_specs=[pl.BlockSpec((1, r, w), lambda i: (i, 0, 0)), _const_spec((2, w)),
                  _const_spec((w, hidden)), _const_spec((w, hidden)), _const_spec((hidden, LANES))],
        out_specs=pl.BlockSpec((1,) + out_dims, lambda i: (i, 0, 0)),
        compiler_params=_cparams(1),
        name="nsa_compress",
    )(t, pos.reshape(2, w), w1a, w1b, w2d)


def _nsa_cmp_body(q_ref, kc_ref, vct_ref, ovt_ref, oc_ref, bias_ref):
    qi = pl.program_id(2)
    tq = q_ref.shape[1]
    r = kc_ref.shape[1]
    nsel = ovt_ref.shape[0]
    kc_terms = _split_bf16(kc_ref[0], 2)
    vct_b = vct_ref[0].astype(BF16)
    lo = _lane_iota(tq) < HEAD_DIM
    n = lax.broadcasted_iota(jnp.int32, (r, tq), 0)
    pos = qi * tq + lax.broadcasted_iota(jnp.int32, (r, tq), 1)
    ok = (n * NSA_CMP_STRIDE + (NSA_CMP_LEN - 1) <= pos) & (n < r - 1)
    psum = jnp.zeros((r, tq), F32)
    outs = []
    for hd in range(NSA_GROUP):
        qp = q_ref[0, :, (hd // 2) * LANES:(hd // 2 + 1) * LANES]
        qm = jnp.where(lo if hd % 2 == 0 else ~lo, qp, 0.0)
        s = _dot_nt_3pass(kc_terms, qm) * SCALE
        s = jnp.where(ok, s, NEG_INF)
        e = jnp.where(ok, jnp.exp(s - jnp.max(s, axis=0, keepdims=True)), 0.0)
        l = jnp.sum(e, axis=0, keepdims=True)
        p = e * jnp.where(l > 0.0, 1.0 / l, 0.0)
        psum = psum + p
        outs.append(jnp.dot(vct_b, p.astype(BF16), preferred_element_type=F32))
    even_rows = lax.broadcasted_iota(jnp.int32, (LANES, tq), 0) < HEAD_DIM
    for pr in range(NSA_GROUP // 2):
        oc_ref[0, :, pr * LANES:(pr + 1) * LANES] = jnp.where(even_rows, outs[2 * pr], outs[2 * pr + 1]).T
    ovt = ovt_ref[...].astype(BF16)
    imp = sum(jnp.dot(ovt, t, preferred_element_type=F32) for t in _split_bf16(psum, 3))
    blk = lax.broadcasted_iota(jnp.int32, (nsel, tq), 0)
    cur = (qi * tq + lax.broadcasted_iota(jnp.int32, (nsel, tq), 1)) // NSA_SEL_BLOCK
    ok_blk = blk <= cur
    forced = ok_blk & ((blk == 0) | (blk >= cur - 1))
    score = jnp.where(ok_blk, jnp.where(forced, NSA_FORCE, imp), NEG_INF)
    chosen = _topk_rows(score, min(NSA_SEL_TOPK, nsel)) & ok_blk
    bias_ref[0, 0] = _lane_spread(jnp.where(chosen, 1.0, 0.0))


def _nsa_overlap_t(r, nsel):
    cs = np.arange(r) * NSA_CMP_STRIDE
    ss = np.arange(nsel) * NSA_SEL_BLOCK
    ov = (np.minimum(cs[None, :] + NSA_CMP_LEN, ss[:, None] + NSA_SEL_BLOCK)
          - np.maximum(cs[None, :], ss[:, None]))
    return jnp.asarray(np.clip(ov, 0, None) / NSA_CMP_LEN, dtype=F32)


def _nsa_cmp_attention(nq, kc, vct, batch, seq):
    g = NSA_KV_HEADS
    tq = min(CMP_TILE, seq)
    r = kc.shape[1]
    nsel = seq // NSA_SEL_BLOCK
    gw = NSA_GROUP * HEAD_DIM
    return pl.pallas_call(
        _nsa_cmp_body,
        out_shape=[jax.ShapeDtypeStruct((batch, seq, g * gw), F32),
                   jax.ShapeDtypeStruct((batch, g, seq, LANES), F32)],
        grid=(batch, g, seq // tq),
        in_specs=[pl.BlockSpec((1, tq, gw), lambda b, gi, i: (b, i, gi)),
                  pl.BlockSpec((1, r, LANES), lambda b, gi, i: (b * g + gi, 0, 0)),
                  pl.BlockSpec((1, LANES, r), lambda b, gi, i: (b * g + gi, 0, 0)),
                  _const_spec((nsel, r))],
        out_specs=[pl.BlockSpec((1, tq, gw), lambda b, gi, i: (b, i, gi)),
                   pl.BlockSpec((1, 1, tq, LANES), lambda b, gi, i: (b, gi, i, 0))],
        compiler_params=_cparams(3),
        name="nsa_compressed_attention",
    )(nq, kc, vct, _nsa_overlap_t(r, nsel))


def _nsa_main_body(q_ref, ksa_ref, ksb_ref, vsa_ref, vsb_ref, kw_ref, vwa_ref, vwb_ref,
                   bias_ref, gt_ref, oc_ref, o_ref, acc_ref):
    gi = pl.program_id(1)
    qi = pl.program_id(2)
    tq = q_ref.shape[1]
    kt = min(K_TILE, ksa_ref.shape[2])
    lane = _lane_iota(tq)
    lo = lane < HEAD_DIM
    bias = bias_ref[0, 0]
    q_sel, q_win = [], []
    for hd in range(NSA_GROUP):
        qs = q_ref[0, :, (hd // 2) * LANES:(hd // 2 + 1) * LANES] * QK_SCALE
        half = lo if hd % 2 == 0 else ~lo
        q_sel.append(jnp.where(half, qs, bias).astype(BF16))
        q_win.append(jnp.where(half, qs, 0.0).astype(BF16))
    k_sel = [r.at[0] for r in (ksa_ref, ksb_ref)]
    v_sel = [r.at[0] for r in (vsa_ref, vsb_ref)]
    v_win = [r.at[0] for r in (vwa_ref, vwb_ref)]

    def tile(j, ms, width, mask, qs, ks, vs, align=None):
        rows = pl.ds(pl.multiple_of(j * (align or width), align or width), width)
        s = [lax.dot_general(qs[hd], ks[hd % 2][0, rows, :], NT, preferred_element_type=F32)
             for hd in range(NSA_GROUP)]
        out = []
        for hd in range(NSA_GROUP):
            m, alpha, p = _softmax_step(ms[hd], s[hd], mask)
            _accumulate(acc_ref.at[hd], alpha, p, vs[hd % 2][0, rows, :])
            out.append(m)
        return tuple(out)

    m0 = (jnp.full((tq, 1), M_INIT, F32),) * NSA_GROUP

    acc_ref[...] = jnp.zeros_like(acc_ref)
    _causal_sweep(qi, tq, kt, functools.partial(tile, qs=q_sel, ks=k_sel, vs=v_sel), m0)
    o_sel = [_normalized(acc_ref[hd]) for hd in range(NSA_GROUP)]

    acc_ref[...] = jnp.zeros_like(acc_ref)
    k_win = [kw_ref.at[0]] * 2
    span = NSA_WINDOW + tq
    first = jnp.maximum(qi * tq - NSA_WINDOW, 0)
    delta = (qi * tq - first + lax.broadcasted_iota(jnp.int32, (tq, span), 0)
             - lax.broadcasted_iota(jnp.int32, (tq, span), 1))
    tile(first // tq, m0, span, _mask_bias((delta >= 0) & (delta < NSA_WINDOW)), q_win, k_win, v_win, align=tq)
    o_win = [_normalized(acc_ref[hd]) for hd in range(NSA_GROUP)]

    gt = gt_ref[0]

    def gate(branch, hd):
        c = branch * NSA_HEADS + gi * NSA_GROUP + hd
        return jnp.sum(jnp.where(lane == c, gt, 0.0), axis=-1, keepdims=True)

    for pr in range(NSA_GROUP // 2):
        a, b = 2 * pr, 2 * pr + 1
        o = (jnp.where(lo, gate(0, a), gate(0, b)) * oc_ref[0, :, pr * LANES:(pr + 1) * LANES]
             + jnp.where(lo, gate(1, a) * o_sel[a], gate(1, b) * o_sel[b])
             + jnp.where(lo, gate(2, a) * o_win[a], gate(2, b) * o_win[b]))
        o_ref[0, :, pr * LANES:(pr + 1) * LANES] = o.astype(o_ref.dtype)


def _nsa_main_attention(nq, ksa, ksb, vsa, vsb, kw, vwa, vwb, bias, gates, oc):
    batch, seq, w = nq.shape
    g = NSA_KV_HEADS
    tq = min(Q_TILE, seq)
    gw = NSA_GROUP * HEAD_DIM
    kv_spec = pl.BlockSpec((1, 1, seq, LANES), lambda b, gi, i: (b, gi, 0, 0))
    q_spec = pl.BlockSpec((1, tq, gw), lambda b, gi, i: (b, i, gi))
    return pl.pallas_call(
        _nsa_main_body,
        out_shape=jax.ShapeDtypeStruct((batch, seq, w), BF16),
        grid=(batch, g, seq // tq),
        in_specs=[q_spec] + [kv_spec] * 7
                 + [pl.BlockSpec((1, 1, tq, LANES), lambda b, gi, i: (b, gi, i, 0)),
                    pl.BlockSpec((1, tq, LANES), lambda b, gi, i: (b, i, 0)),
                    q_spec],
        out_specs=q_spec,
        scratch_shapes=[pltpu.VMEM((NSA_GROUP, tq, LANES), F32)],
        compiler_params=_cparams(3),
        name="nsa_selected_window_attention",
    )(nq, ksa, ksb, vsa, vsb, kw, vwa, vwb, bias, gates, oc)


def _diff_proj_body(x_ref, an_ref, sc_ref, sh_ref, w_ref, cos_ref, sin_ref, hg_ref,
                    q_ref, k_ref, v_ref):
    h = _norm_mod(x_ref[...], an_ref[...], sc_ref[0], sh_ref[0]).astype(BF16)
    cos = cos_ref[...]
    sin = sin_ref[...]
    w = q_ref.shape[1]
    chunk = 4 * LANES
    for c0 in range(0, w, chunk):
        yq = jnp.dot(h, w_ref[:, c0:c0 + chunk], preferred_element_type=F32)
        yk = jnp.dot(h, w_ref[:, w + c0:w + c0 + chunk], preferred_element_type=F32)
        yv = jnp.dot(h, w_ref[:, 2 * w + c0:2 * w + c0 + chunk], preferred_element_type=F32)
        sq, sk = _head_sumsq(yq), _head_sumsq(yk)
        for p in range(chunk // LANES):
            sl = slice(p * LANES, (p + 1) * LANES)
            dst = slice(c0 + p * LANES, c0 + (p + 1) * LANES)
            q_ref[:, dst] = (_head_norm_rope(yq[:, sl], sq[:, sl], hg_ref[0:1, :], cos, sin) * QK_SCALE).astype(q_ref.dtype)
            k_ref[:, dst] = _head_norm_rope(yk[:, sl], sk[:, sl], hg_ref[1:2, :], cos, sin).astype(k_ref.dtype)
        v_ref[:, c0:c0 + chunk] = yv.astype(v_ref.dtype)


def _diff_proj(x2, an, sc, sh, w, cos, sin, hgains, seq):
    t, d = x2.shape
    rows = min(PROJ_ROWS, seq)
    per_b = seq // rows
    wq = w.shape[1] // 3
    row_spec = lambda n: pl.BlockSpec((rows, n), lambda i: (i, 0))
    mod_spec = pl.BlockSpec((1, 1, d), lambda i: (i // per_b, 0, 0))
    return pl.pallas_call(
        _diff_proj_body,
        out_shape=[jax.ShapeDtypeStruct((t, wq), BF16)] * 3,
        grid=(t // rows,),
        in_specs=[row_spec(d), _const_spec((1, d)), mod_spec, mod_spec, _const_spec(w.shape),
                  row_spec(LANES), row_spec(LANES), _const_spec(hgains.shape)],
        out_specs=[row_spec(wq)] * 3,
        compiler_params=_cparams(1),
        name="diff_in_proj",
    )(x2, an, sc, sh, w, cos, sin, hgains)


def _diff_body(lam_ref, on_ref, q_ref, k_ref, v_ref, o_ref, acc_ref, den_ref, *, lam_init):
    qi = pl.program_id(2)
    tq = q_ref.shape[1]
    kt = min(K_TILE, k_ref.shape[1])
    heads = q_ref.shape[2] // LANES
    lo = _lane_iota(tq) < HEAD_DIM
    qb = []
    for hd in range(heads):
        q = q_ref[0, :, hd * LANES:(hd + 1) * LANES]
        zero = jnp.zeros_like(q)
        qb += [jnp.where(lo, q, zero), jnp.where(lo, zero, q)]
    acc_ref[...] = jnp.zeros_like(acc_ref)
    den_ref[...] = jnp.zeros_like(den_ref)

    def tile(j, ms, width, mask):
        k = _row_tile(k_ref, j, width)
        v = _row_tile(v_ref, j, width)
        s = [lax.dot_general(qb[c], k[:, (c // 2) * LANES:(c // 2 + 1) * LANES], NT,
                             preferred_element_type=F32) for c in range(2 * heads)]
        out = []
        for c in range(2 * heads):
            m, alpha, p = _softmax_step(ms[c], s[c], mask)
            parts = [p[:, g * LANES:(g + 1) * LANES] for g in range(width // LANES)]
            while len(parts) > 1:
                parts = [a + b for a, b in zip(parts[0::2], parts[1::2])]
            den_ref[c] = alpha * den_ref[c] + parts[0].astype(F32)
            _accumulate(acc_ref.at[c], alpha, p, v[:, (c // 2) * LANES:(c // 2 + 1) * LANES])
            out.append(m)
        return tuple(out)

    m0 = jnp.full((tq, 1), M_INIT, F32)
    _causal_sweep(qi, tq, kt, tile, (m0,) * (2 * heads))
    lp = lam_ref[...]
    lam = (jnp.exp(jnp.sum(lp[0:1] * lp[1:2], axis=-1, keepdims=True))
           - jnp.exp(jnp.sum(lp[2:3] * lp[3:4], axis=-1, keepdims=True)) + lam_init)
    for hd in range(heads):
        den = [jnp.sum(den_ref[2 * hd + c], axis=-1, keepdims=True) for c in range(2)]
        o = acc_ref[2 * hd] / den[0] - lam * (acc_ref[2 * hd + 1] / den[1])
        ms = jnp.mean(o * o, axis=-1, keepdims=True)
        o = o * lax.rsqrt(ms + NORM_EPS) * on_ref[...] * (1.0 - lam_init)
        o_ref[0, :, hd * LANES:(hd + 1) * LANES] = o.astype(o_ref.dtype)


def _diff_attention(q, k, v, lam_params, out_norm, lam_init):
    batch, seq, w = q.shape
    tq = min(Q_TILE, seq)
    gw = DIFF_HEADS_PER_STEP * LANES
    kv_spec = pl.BlockSpec((1, seq, gw), lambda b, h, i: (b, 0, h))
    q_spec = pl.BlockSpec((1, tq, gw), lambda b, h, i: (b, i, h))
    return pl.pallas_call(
        functools.partial(_diff_body, lam_init=lam_init),
        out_shape=jax.ShapeDtypeStruct((batch, seq, w), BF16),
        grid=(batch, w // gw, seq // tq),
        in_specs=[_const_spec(lam_params.shape), _const_spec((1, LANES)), q_spec, kv_spec, kv_spec],
        out_specs=q_spec,
        scratch_shapes=[pltpu.VMEM((2 * DIFF_HEADS_PER_STEP, tq, LANES), F32)] * 2,
        compiler_params=_cparams(3),
        name="diff_attention",
    )(lam_params, out_norm.reshape(1, LANES), q, k, v)


def _out_ffn_body(*refs, n_attn):
    x_ref = refs[0]
    attn = refs[1:1 + 2 * n_attn]
    ga_ref, fn_ref, sc_ref, sh_ref, gf_ref, wg_ref, wu_ref, wd_ref, o_ref = refs[1 + 2 * n_attn:]
    y = None
    for a in range(n_attn):
        part = jnp.dot(attn[2 * a][...], attn[2 * a + 1][...], preferred_element_type=F32)
        y = part if y is None else y + part
    x1 = x_ref[...] + ga_ref[0] * y
    h = _norm_mod(x1, fn_ref[...], sc_ref[0], sh_ref[0]).astype(BF16)
    acc = jnp.zeros(x1.shape, F32)
    for c0 in range(0, wg_ref.shape[1], FFN_CHUNK):
        gch = jnp.dot(h, wg_ref[:, c0:c0 + FFN_CHUNK], preferred_element_type=F32)
        uch = jnp.dot(h, wu_ref[:, c0:c0 + FFN_CHUNK], preferred_element_type=F32)
        act = (_silu(gch) * uch).astype(BF16)
        acc = acc + jnp.dot(act, wd_ref[c0:c0 + FFN_CHUNK, :], preferred_element_type=F32)
    o_ref[...] = x1 + gf_ref[0] * acc


def _out_ffn(x2, attn_pairs, ga, fn, sc, sh, gf, wg, wu, wd, seq):
    t, d = x2.shape
    rows = min(PROJ_ROWS, seq)
    per_b = seq // rows
    row_spec = lambda n: pl.BlockSpec((rows, n), lambda i: (i, 0))
    mod_spec = pl.BlockSpec((1, 1, d), lambda i: (i // per_b, 0, 0))
    resident = lambda a: pl.BlockSpec(a.shape, lambda i: (0, 0), pipeline_mode=pl.Buffered(1))
    args, specs = [x2], [row_spec(d)]
    for o, w in attn_pairs:
        args += [o, w]
        specs += [row_spec(o.shape[1]), resident(w)]
    args += [ga, fn, sc, sh, gf, wg, wu, wd]
    specs += [mod_spec, _const_spec((1, d)), mod_spec, mod_spec, mod_spec,
              resident(wg), resident(wu), resident(wd)]
    return pl.pallas_call(
        functools.partial(_out_ffn_body, n_attn=len(attn_pairs)),
        out_shape=jax.ShapeDtypeStruct((t, d), F32),
        grid=(t // rows,),
        in_specs=specs,
        out_specs=row_spec(d),
        compiler_params=_cparams(1),
        name="out_proj_swiglu",
    )(*args)


def _rope_lane_tables(positions):
    inv_freq = 1.0 / (ROPE_THETA ** (jnp.arange(0, ROT_DIM, 2, dtype=F32) / ROT_DIM))
    lane = np.arange(LANES) % HEAD_DIM
    rotary = lane < ROT_DIM
    freq = jnp.where(rotary, inv_freq[lane % ROT_HALF], 0.0)
    sign = jnp.asarray(np.where(rotary, np.where(lane < ROT_HALF, -1.0, 1.0), 0.0), F32)
    ang = positions.astype(F32).reshape(-1, 1) * freq
    return jnp.where(rotary, jnp.cos(ang), 1.0), jnp.sin(ang) * sign


def _pair_gain(gain):
    return jnp.concatenate([gain, gain]).reshape(1, LANES)


def _pad_rows(a, rows=8):
    return jnp.concatenate([a, jnp.zeros((rows - a.shape[0], a.shape[1]), a.dtype)], axis=0)


def kernel(x, c, positions, ada_w, ada_b, attn_norm, ffn_norm, ffn_w_gate, ffn_w_up, ffn_w_down, sp_w_in, sp_w_out, moba_q_norm, moba_k_norm, nsa_q_norm, nsa_k_norm, nsa_cmp_pos, nsa_cmp_w1, nsa_cmp_w2, diff_w_in, diff_w_out, diff_q_norm, diff_k_norm, diff_lambda, diff_out_norm):
    batch, seq, d = x.shape
    depth = ada_w.shape[0]
    assert seq % K_TILE == 0 and K_TILE % Q_TILE == 0 and Q_TILE == TAIL_TILE
    assert seq % MOBA_BLOCK == 0 and NSA_WINDOW % Q_TILE == 0 and NSA_WINDOW + Q_TILE <= seq
    assert (seq // MOBA_BLOCK) % SUBLANES == 0 and (seq // NSA_SEL_BLOCK) % SUBLANES == 0
    assert seq // MOBA_BLOCK <= HEAD_DIM and seq // NSA_SEL_BLOCK <= HEAD_DIM
    g = NSA_KV_HEADS
    cos, sin = _rope_lane_tables(positions)
    mod = _modulation(c, ada_w, ada_b).reshape(depth, batch, 6, 1, d)
    x2 = x.reshape(batch * seq, d)
    for i in range(depth):
        sh_a, sc_a, g_a, sh_f, sc_f, g_f = (mod[i, :, n] for n in range(6))
        an = attn_norm[i].reshape(1, d)
        j = i // 2
        to3 = lambda a: a.reshape(batch, seq, a.shape[-1])
        if i % 2 == 0:
            w_in = sp_w_in[j]
            pad = (-w_in.shape[1]) % LANES
            w_pad = jnp.concatenate([w_in, jnp.zeros((d, pad), w_in.dtype)], axis=1).astype(BF16)
            hgains = _pad_rows(jnp.concatenate(
                [_pair_gain(moba_q_norm[j]), _pair_gain(moba_k_norm[j]), _pair_gain(nsa_q_norm[j])]
                + [_pair_gain(nsa_k_norm[j, n]) for n in range(3)], axis=0))
            (mq, mka, mkb, mva, mvb, km, nq, ck, cv, ksa, ksb, vsa, vsb, kw, vwa, vwb, gates) = _sparse_proj(
                x2, an, sc_a, sh_a, w_pad, cos, sin, hgains, batch, seq)
            w512 = MOBA_HEADS * HEAD_DIM
            o_moba = _moba_attention(to3(mq), to3(mka), to3(mkb), to3(mva), to3(mvb),
                                     km.reshape(batch, seq // MOBA_BLOCK, w512))
            r = seq // NSA_CMP_STRIDE
            cw = NSA_CMP_STRIDE * HEAD_DIM
            kc = _compress(ck.reshape(batch * g, r, cw), nsa_cmp_pos[j, 0], nsa_cmp_w1[j, 0], nsa_cmp_w2[j, 0], False)
            vct = _compress(cv.reshape(batch * g, r, cw), nsa_cmp_pos[j, 1], nsa_cmp_w1[j, 1], nsa_cmp_w2[j, 1], True)
            oc, bias = _nsa_cmp_attention(to3(nq), kc, vct, batch, seq)
            o_nsa = _nsa_main_attention(to3(nq), ksa, ksb, vsa, vsb, kw, vwa, vwb, bias, to3(gates), oc)
            w_out = sp_w_out[j].astype(BF16)
            attn_pairs = [(o_moba.reshape(batch * seq, w512), w_out[:w512]),
                          (o_nsa.reshape(batch * seq, w512), w_out[w512:])]
        else:
            lam_init = 0.8 - 0.6 * float(np.exp(-0.3 * i))
            hgains = _pad_rows(jnp.concatenate(
                [_pair_gain(diff_q_norm[j]), _pair_gain(diff_k_norm[j])], axis=0))
            q, k, v = _diff_proj(x2, an, sc_a, sh_a, diff_w_in[j].astype(BF16), cos, sin, hgains, seq)
            o = _diff_attention(to3(q), to3(k), to3(v), diff_lambda[j], diff_out_norm[j], lam_init)
            attn_pairs = [(o.reshape(batch * seq, o.shape[-1]), diff_w_out[j].astype(BF16))]
        x2 = _out_ffn(x2, attn_pairs, g_a, ffn_norm[i].reshape(1, d), sc_f, sh_f, g_f,
                      ffn_w_gate[i].astype(BF16), ffn_w_up[i].astype(BF16),
                      ffn_w_down[i].astype(BF16), seq)
    return x2.reshape(batch, seq, d)
```

```python
import functools

import numpy as np
import jax
import jax.numpy as jnp
from jax import lax
from jax.experimental import pallas as pl
from jax.experimental.pallas import tpu as pltpu

F32 = jnp.float32
BF16 = jnp.bfloat16
HI = lax.Precision.HIGHEST

LANES = 128
HEAD_DIM = 64
ROT_DIM = HEAD_DIM // 4
ROT_HALF = ROT_DIM // 2
ROPE_THETA = 500000.0
NORM_EPS = 1e-6
NEG_INF = -1e30
M_INIT = -1e20
SCALE = HEAD_DIM ** -0.5
QK_SCALE = SCALE * 1.4426950408889634

MOBA_HEADS = 8
MOBA_BLOCK = 256
MOBA_TOPK = 3

NSA_HEADS = 8
NSA_KV_HEADS = 2
NSA_GROUP = NSA_HEADS // NSA_KV_HEADS
NSA_CMP_LEN = 32
NSA_CMP_STRIDE = 16
NSA_SEL_BLOCK = 64
NSA_SEL_TOPK = 16
NSA_WINDOW = 512
NSA_FORCE = 1e4

DIFF_HEADS = 8
DIFF_HEADS_PER_STEP = 2
MOBA_PAIRS_PER_STEP = 2

Q_TILE = 512
K_TILE = 1024
TAIL_TILE = 512
CMP_TILE = 512
SUBLANES = 8
PROJ_ROWS = 512
FFN_CHUNK = 256
VMEM_LIMIT = 56 * 2 ** 20

NT = (((1,), (1,)), ((), ()))


def _cparams(n_axes):
    return pltpu.CompilerParams(dimension_semantics=("arbitrary",) * n_axes,
                                vmem_limit_bytes=VMEM_LIMIT)


def _const_spec(shape):
    return pl.BlockSpec(shape, lambda *_: (0,) * len(shape))


def _silu(t):
    return t / (1.0 + jnp.exp(-t))


def _sigmoid(t):
    return 1.0 / (1.0 + jnp.exp(-t))


def _lane_iota(rows):
    return lax.broadcasted_iota(jnp.int32, (rows, LANES), 1)


def _mod_body(c_ref, w_ref, b_ref, o_ref):
    a = _silu(c_ref[...])
    o_ref[0] = jnp.dot(a, w_ref[0], precision=HI, preferred_element_type=F32) + b_ref[0]


def _modulation(c, ada_w, ada_b):
    depth, d, n = ada_w.shape
    b = c.shape[0]
    tn = n // 4
    return pl.pallas_call(
        _mod_body,
        out_shape=jax.ShapeDtypeStruct((depth, b, n), F32),
        grid=(depth, n // tn),
        in_specs=[pl.BlockSpec((b, d), lambda i, j: (0, 0)),
                  pl.BlockSpec((1, d, tn), lambda i, j: (i, 0, j)),
                  pl.BlockSpec((1, 1, tn), lambda i, j: (i, 0, j))],
        out_specs=pl.BlockSpec((1, b, tn), lambda i, j: (i, 0, j)),
        compiler_params=_cparams(2),
        name="adaln_modulation",
    )(c, ada_w, ada_b.reshape(depth, 1, n))


def _norm_mod(x, gain, sc, sh):
    ms = jnp.mean(x * x, axis=-1, keepdims=True)
    return x * lax.rsqrt(ms + NORM_EPS) * gain * (1.0 + sc) + sh


def _head_sumsq(y):
    wide = 2 * LANES
    same_head = (lax.broadcasted_iota(jnp.int32, (wide, wide), 0) // HEAD_DIM
                 == lax.broadcasted_iota(jnp.int32, (wide, wide), 1) // HEAD_DIM)
    ones_bd = jnp.where(same_head, 1.0, 0.0).astype(BF16)
    y2 = (y * y).astype(BF16)
    out = [jnp.dot(y2[:, c:c + wide], ones_bd, preferred_element_type=F32)
           for c in range(0, y.shape[1], wide)]
    return out[0] if len(out) == 1 else jnp.concatenate(out, axis=1)


def _head_norm_rope(y, ss, gain, cos, sin):
    rows = y.shape[0]
    lane = _lane_iota(rows)
    yn = y * lax.rsqrt(ss * (1.0 / HEAD_DIM) + NORM_EPS) * gain
    ahead = pltpu.roll(yn, LANES - ROT_HALF, 1)
    behind = pltpu.roll(yn, ROT_HALF, 1)
    partner = jnp.where((lane & (HEAD_DIM - 1)) < ROT_HALF, ahead, behind)
    return yn * cos + partner * sin


def _split_bf16(a, terms):
    out = []
    for _ in range(terms):
        t = a.astype(BF16)
        out.append(t)
        a = a - t.astype(F32)
    return out


def _dot_nt_3pass(a_terms, b):
    a_hi, a_lo = a_terms
    b_hi, b_lo = _split_bf16(b, 2)
    d = lambda u, v: lax.dot_general(u, v, NT, preferred_element_type=F32)
    return d(a_hi, b_hi) + (d(a_hi, b_lo) + d(a_lo, b_hi))


def _dot_3pass(a, b):
    a_hi, a_lo = _split_bf16(a, 2)
    b_hi, b_lo = _split_bf16(b, 2)
    d = lambda u, v: jnp.dot(u, v, preferred_element_type=F32)
    return d(a_hi, b_hi) + (d(a_hi, b_lo) + d(a_lo, b_hi))


def _softmax_step(m, s, bias):
    sb = s.astype(BF16)
    if bias is not None:
        sb = sb + bias
    m_new = jnp.maximum(m, jnp.max(sb, axis=-1, keepdims=True).astype(F32))
    return m_new, jnp.exp2(m - m_new), jnp.exp2(sb - m_new.astype(BF16))


def _mask_bias(mask):
    return jnp.where(mask, 0.0, NEG_INF).astype(BF16)


def _accumulate(acc_ref, alpha, p, v):
    acc_ref[...] = alpha * acc_ref[...] + jnp.dot(p, v, preferred_element_type=F32)


def _normalized(acc):
    return acc / pltpu.roll(acc, HEAD_DIM, 1)


def _topk_rows(score, k):
    n, cols = score.shape
    nblk = n // SUBLANES
    blocks = [score[b * SUBLANES:(b + 1) * SUBLANES] for b in range(nblk)]
    sub = lax.broadcasted_iota(jnp.int32, (SUBLANES, cols), 0)
    ranks = [jnp.zeros((SUBLANES, cols), jnp.int32) for _ in range(nblk)]
    for i in range(n):
        si = score[i:i + 1, :]
        for b in range(nblk):
            if b > i // SUBLANES:
                beats = si >= blocks[b]
            elif b < i // SUBLANES:
                beats = si > blocks[b]
            else:
                beats = (si > blocks[b]) | ((si == blocks[b]) & (sub > i % SUBLANES))
            ranks[b] = ranks[b] + jnp.where(beats, 1, 0)
    return jnp.concatenate(ranks, axis=0) < k


def _lane_spread(allowed):
    n, cols = allowed.shape
    parts = [allowed, allowed]
    if n < HEAD_DIM:
        fill = jnp.zeros((HEAD_DIM - n, cols), F32)
        parts = [allowed, fill, allowed, fill]
    return jnp.where(jnp.concatenate(parts, axis=0).T > 0.5, 0.0, NEG_INF)


def _row_tile(ref, j, rows):
    return ref[0, pl.ds(pl.multiple_of(j * rows, rows), rows), :]


def _key_rows(j, width, align):
    align = align or width
    return pl.ds(pl.multiple_of(j * align, align), width)


def _causal_sweep(qi, tq, kt, tile, state):
    n_wide = (qi * tq) // kt
    state = lax.fori_loop(0, n_wide, lambda j, c: tile(j, c, kt, None), state)
    state = lax.fori_loop(n_wide * (kt // tq), qi, lambda j, c: tile(j, c, tq, None), state)
    half = tq // 2
    for hh in range(2):
        width = (hh + 1) * half
        row = hh * half + lax.broadcasted_iota(jnp.int32, (half, width), 0)
        col = lax.broadcasted_iota(jnp.int32, (half, width), 1)
        tile(qi, state, width, _mask_bias(col <= row), rows=slice(hh * half, (hh + 1) * half), align=tq)


def _sparse_proj_body(x_ref, an_ref, sc_ref, sh_ref, w_ref, cos_ref, sin_ref, hg_ref,
                      mq_ref, mka_ref, mkb_ref, mva_ref, mvb_ref, km_ref, nq_ref, ck_ref, cv_ref,
                      ksa_ref, ksb_ref, vsa_ref, vsb_ref, kw_ref, vwa_ref, vwb_ref, gt_ref,
                      *, per_b):
    rows = x_ref.shape[0]
    h = _norm_mod(x_ref[...], an_ref[...], sc_ref[0], sh_ref[0]).astype(BF16)
    cos = cos_ref[...]
    sin = sin_ref[...]
    lane = _lane_iota(rows)
    lo = lane < HEAD_DIM
    seq_pos = (pl.program_id(0) % per_b) * rows + lax.broadcasted_iota(jnp.int32, (rows, LANES), 0)
    lane_blk = lane & (HEAD_DIM - 1)
    moba_hot = jnp.where(lane_blk == seq_pos // MOBA_BLOCK, 1.0, 0.0)
    sel_hot = jnp.where(lane_blk == seq_pos // NSA_SEL_BLOCK, 1.0, 0.0)

    def proj(c0, n):
        return jnp.dot(h, w_ref[:, c0:c0 + n], preferred_element_type=F32)

    def group(y, p):
        return y[:, p * LANES:(p + 1) * LANES]

    def nr(y, ss, p, gi):
        return _head_norm_rope(group(y, p), group(ss, p), hg_ref[gi:gi + 1, :], cos, sin)

    def split_groups(y, fill, a_ref, b_ref):
        rolled = pltpu.roll(y, HEAD_DIM, 1)
        a_ref[0, 0] = jnp.where(lo, y, fill).astype(a_ref.dtype)
        b_ref[0, 0] = jnp.where(lo, fill, rolled).astype(b_ref.dtype)
        a_ref[0, 1] = jnp.where(lo, rolled, fill).astype(a_ref.dtype)
        b_ref[0, 1] = jnp.where(lo, fill, y).astype(b_ref.dtype)

    moba_w = MOBA_HEADS * HEAD_DIM
    pairs = moba_w // LANES
    y = proj(0, moba_w)
    ss = _head_sumsq(y)
    for p in range(pairs):
        mq_ref[:, p * LANES:(p + 1) * LANES] = nr(y, ss, p, 0)
    y = proj(moba_w, moba_w)
    ss = _head_sumsq(y)
    for p in range(pairs):
        sl = slice(p * LANES, (p + 1) * LANES)
        k = nr(y, ss, p, 1)
        mka_ref[:, sl] = jnp.where(lo, k, moba_hot).astype(mka_ref.dtype)
        mkb_ref[:, sl] = jnp.where(lo, moba_hot, k).astype(mkb_ref.dtype)
        for r in range(rows // MOBA_BLOCK):
            km_ref[r, :, sl] = jnp.mean(k[r * MOBA_BLOCK:(r + 1) * MOBA_BLOCK], axis=0, keepdims=True)
    y = proj(2 * moba_w, moba_w)
    for p in range(pairs):
        sl = slice(p * LANES, (p + 1) * LANES)
        mva_ref[:, sl] = jnp.where(lo, group(y, p), 1.0).astype(mva_ref.dtype)
        mvb_ref[:, sl] = jnp.where(lo, 1.0, group(y, p)).astype(mvb_ref.dtype)
    c0 = 3 * moba_w
    y = proj(c0, NSA_HEADS * HEAD_DIM)
    ss = _head_sumsq(y)
    for p in range(NSA_HEADS * HEAD_DIM // LANES):
        nq_ref[:, p * LANES:(p + 1) * LANES] = nr(y, ss, p, 2)
    c0 += NSA_HEADS * HEAD_DIM
    y = proj(c0, 6 * LANES)
    ss = _head_sumsq(y)
    kc = nr(y, ss, 0, 3)
    ck_ref[0, 0] = kc[:, :HEAD_DIM]
    ck_ref[0, 1] = kc[:, HEAD_DIM:]
    vc = group(y, 1)
    cv_ref[0, 0] = vc[:, :HEAD_DIM]
    cv_ref[0, 1] = vc[:, HEAD_DIM:]
    split_groups(nr(y, ss, 2, 4), sel_hot, ksa_ref, ksb_ref)
    split_groups(group(y, 3), 1.0, vsa_ref, vsb_ref)
    kw = nr(y, ss, 4, 5)
    kw_rolled = pltpu.roll(kw, HEAD_DIM, 1)
    kw_ref[0, 0] = jnp.where(lo, kw, kw_rolled).astype(kw_ref.dtype)
    kw_ref[0, 1] = jnp.where(lo, kw_rolled, kw).astype(kw_ref.dtype)
    split_groups(group(y, 5), 1.0, vwa_ref, vwb_ref)
    c0 += 6 * LANES
    gt_ref[...] = _sigmoid(proj(c0, LANES))


def _sparse_proj(x2, an, sc, sh, w_pad, cos, sin, hgains, batch, seq):
    t, d = x2.shape
    rows = min(PROJ_ROWS, seq)
    per_b = seq // rows
    nb = seq // MOBA_BLOCK
    g = NSA_KV_HEADS
    w512 = MOBA_HEADS * HEAD_DIM
    row_spec = lambda w: pl.BlockSpec((rows, w), lambda i: (i, 0))
    mod_spec = pl.BlockSpec((1, 1, d), lambda i: (i // per_b, 0, 0))
    grp64 = pl.BlockSpec((1, g, rows, HEAD_DIM), lambda i: (i // per_b, 0, i % per_b, 0))
    grp128 = pl.BlockSpec((1, g, rows, LANES), lambda i: (i // per_b, 0, i % per_b, 0))
    tok = lambda dt: jax.ShapeDtypeStruct((t, w512), dt)
    kv = lambda dt, w: jax.ShapeDtypeStruct((batch, g, seq, w), dt)
    out_shape = [tok(F32), tok(BF16), tok(BF16), tok(BF16), tok(BF16),
                 jax.ShapeDtypeStruct((batch * nb, 1, w512), F32),
                 tok(F32),
                 kv(F32, HEAD_DIM), kv(F32, HEAD_DIM),
                 kv(BF16, LANES), kv(BF16, LANES), kv(BF16, LANES), kv(BF16, LANES),
                 kv(BF16, LANES), kv(BF16, LANES), kv(BF16, LANES),
                 jax.ShapeDtypeStruct((t, LANES), F32)]
    out_specs = ([row_spec(w512)] * 5
                 + [pl.BlockSpec((rows // MOBA_BLOCK, 1, w512), lambda i: (i, 0, 0)), row_spec(w512),
                    grp64, grp64] + [grp128] * 7 + [row_spec(LANES)])
    return pl.pallas_call(
        functools.partial(_sparse_proj_body, per_b=per_b),
        out_shape=out_shape,
        grid=(t // rows,),
        in_specs=[row_spec(d), _const_spec((1, d)), mod_spec, mod_spec,
                  _const_spec(w_pad.shape), row_spec(LANES), row_spec(LANES),
                  _const_spec(hgains.shape)],
        out_specs=out_specs,
        compiler_params=_cparams(1),
        name="sparse_in_proj",
    )(x2, an, sc, sh, w_pad, cos, sin, hgains)


def _moba_body(q_ref, ka_ref, kb_ref, va_ref, vb_ref, km_ref, o_ref, acc_ref):
    qi = pl.program_id(2)
    tq = q_ref.shape[1]
    kt = min(K_TILE, ka_ref.shape[1])
    nb = km_ref.shape[1]
    pairs = q_ref.shape[2] // LANES
    lo = _lane_iota(tq) < HEAD_DIM
    blk = lax.broadcasted_iota(jnp.int32, (nb, tq), 0)
    own = (qi * tq + lax.broadcasted_iota(jnp.int32, (nb, tq), 1)) // MOBA_BLOCK
    past = blk < own
    qp = []
    for pr in range(pairs):
        sl = slice(pr * LANES, (pr + 1) * LANES)
        q = q_ref[0, :, sl]
        km = km_ref[0, :, sl]
        for half in (lo, ~lo):
            gate = lax.dot_general(km, jnp.where(half, q, 0.0), NT, precision=HI, preferred_element_type=F32)
            top = _topk_rows(jnp.where(past, gate, NEG_INF), min(MOBA_TOPK, nb)) & past
            bias = _lane_spread(jnp.where(top | (blk == own), 1.0, 0.0))
            qp.append(jnp.where(half, q * QK_SCALE, bias).astype(BF16))
    acc_ref[...] = jnp.zeros_like(acc_ref)
    k_refs = (ka_ref, kb_ref)
    v_refs = (va_ref, vb_ref)

    def tile(j, ms, width, mask, rows=slice(None), align=None):
        keys = _key_rows(j, width, align)
        ks = [r[0, keys, :] for r in k_refs]
        vs = [r[0, keys, :] for r in v_refs]
        lanes = lambda a, c: a[:, (c // 2) * LANES:(c // 2 + 1) * LANES]
        s = [lax.dot_general(qp[c][rows], lanes(ks[c % 2], c), NT, preferred_element_type=F32)
             for c in range(2 * pairs)]
        out = []
        for c in range(2 * pairs):
            m, alpha, p = _softmax_step(ms[c][rows], s[c], mask)
            _accumulate(acc_ref.at[c, rows], alpha, p, lanes(vs[c % 2], c))
            out.append(m)
        return tuple(out)

    m0 = jnp.full((tq, 1), M_INIT, F32)
    _causal_sweep(qi, tq, kt, tile, (m0,) * (2 * pairs))
    for pr in range(pairs):
        o = jnp.where(lo, _normalized(acc_ref[2 * pr]), _normalized(acc_ref[2 * pr + 1]))
        o_ref[0, :, pr * LANES:(pr + 1) * LANES] = o.astype(o_ref.dtype)


def _moba_attention(mq, mka, mkb, mva, mvb, km):
    batch, seq, w = mq.shape
    tq = min(Q_TILE, seq)
    nb = seq // MOBA_BLOCK
    gw = MOBA_PAIRS_PER_STEP * LANES
    q_spec = pl.BlockSpec((1, tq, gw), lambda b, p, i: (b, i, p))
    kv_spec = pl.BlockSpec((1, seq, gw), lambda b, p, i: (b, 0, p))
    return pl.pallas_call(
        _moba_body,
        out_shape=jax.ShapeDtypeStruct((batch, seq, w), BF16),
        grid=(batch, w // gw, seq // tq),
        in_specs=[q_spec, kv_spec, kv_spec, kv_spec, kv_spec,
                  pl.BlockSpec((1, nb, gw), lambda b, p, i: (b, 0, p))],
        out_specs=q_spec,
        scratch_shapes=[pltpu.VMEM((2 * MOBA_PAIRS_PER_STEP, tq, LANES), F32)],
        compiler_params=_cparams(3),
        name="moba_attention",
    )(mq, mka, mkb, mva, mvb, km)


def _compress_body(t_ref, pos_ref, w1a_ref, w1b_ref, w2_ref, o_ref, *, transposed):
    t = t_ref[0]
    r = t.shape[0]
    a = _dot_3pass(t + pos_ref[0:1, :], w1a_ref[...])
    b = _dot_3pass(t + pos_ref[1:2, :], w1b_ref[...])
    hid = _silu(a + pltpu.roll(b, r - 1, 0))
    out = _dot_3pass(hid, w2_ref[...])
    rowi = lax.broadcasted_iota(jnp.int32, out.shape, 0)
    out = jnp.where(rowi < r - 1, out, 0.0)
    o_ref[0] = out.T if transposed else out


def _compress(t, pos, w1, w2, transposed):
    n, r, w = t.shape
    out_dims = (LANES, r) if transposed else (r, LANES)
    half = NSA_CMP_LEN // 2
    hidden = w1.shape[-1]
    w1a = w1[:half].reshape(w, hidden)
    w1b = w1[half:].reshape(w, hidden)
    w2d = jnp.concatenate([w2, w2], axis=1)
    return pl.pallas_call(
        functools.partial(_compress_body, transposed=transposed),
        out_shape=jax.ShapeDtypeStruct((n,) + out_dims, F32),
        grid=(n,),
        in_specs=[pl.BlockSpec((1, r, w), lambda i: (i, 0, 0)), _const_spec((2, w)),
                  _const_spec((w, hidden)), _const_spec((w, hidden)), _const_spec((hidden, LANES))],
        out_specs=pl.BlockSpec((1,) + out_dims, lambda i: (i, 0, 0)),
        compiler_params=_cparams(1),
        name="nsa_compress",
    )(t, pos.reshape(2, w), w1a, w1b, w2d)


def _nsa_cmp_body(q_ref, kc_ref, vct_ref, ovt_ref, oc_ref, bias_ref):
    qi = pl.program_id(2)
    tq = q_ref.shape[1]
    r = kc_ref.shape[1]
    nsel = ovt_ref.shape[0]
    kc_terms = _split_bf16(kc_ref[0], 2)
    vct_b = vct_ref[0].astype(BF16)
    lo = _lane_iota(tq) < HEAD_DIM
    n = lax.broadcasted_iota(jnp.int32, (r, tq), 0)
    pos = qi * tq + lax.broadcasted_iota(jnp.int32, (r, tq), 1)
    ok = (n * NSA_CMP_STRIDE + (NSA_CMP_LEN - 1) <= pos) & (n < r - 1)
    psum = jnp.zeros((r, tq), F32)
    outs = []
    for hd in range(NSA_GROUP):
        qp = q_ref[0, :, (hd // 2) * LANES:(hd // 2 + 1) * LANES]
        qm = jnp.where(lo if hd % 2 == 0 else ~lo, qp, 0.0)
        s = _dot_nt_3pass(kc_terms, qm) * SCALE
        s = jnp.where(ok, s, NEG_INF)
        e = jnp.where(ok, jnp.exp(s - jnp.max(s, axis=0, keepdims=True)), 0.0)
        l = jnp.sum(e, axis=0, keepdims=True)
        p = e * jnp.where(l > 0.0, 1.0 / l, 0.0)
        psum = psum + p
        outs.append(jnp.dot(vct_b, p.astype(BF16), preferred_element_type=F32))
    even_rows = lax.broadcasted_iota(jnp.int32, (LANES, tq), 0) < HEAD_DIM
    for pr in range(NSA_GROUP // 2):
        oc_ref[0, :, pr * LANES:(pr + 1) * LANES] = jnp.where(even_rows, outs[2 * pr], outs[2 * pr + 1]).T
    ovt = ovt_ref[...].astype(BF16)
    imp = sum(jnp.dot(ovt, t, preferred_element_type=F32) for t in _split_bf16(psum, 3))
    blk = lax.broadcasted_iota(jnp.int32, (nsel, tq), 0)
    cur = (qi * tq + lax.broadcasted_iota(jnp.int32, (nsel, tq), 1)) // NSA_SEL_BLOCK
    ok_blk = blk <= cur
    forced = ok_blk & ((blk == 0) | (blk >= cur - 1))
    score = jnp.where(ok_blk, jnp.where(forced, NSA_FORCE, imp), NEG_INF)
    chosen = _topk_rows(score, min(NSA_SEL_TOPK, nsel)) & ok_blk
    bias_ref[0, 0] = _lane_spread(jnp.where(chosen, 1.0, 0.0))


def _nsa_overlap_t(r, nsel):
    cs = np.arange(r) * NSA_CMP_STRIDE
    ss = np.arange(nsel) * NSA_SEL_BLOCK
    ov = (np.minimum(cs[None, :] + NSA_CMP_LEN, ss[:, None] + NSA_SEL_BLOCK)
          - np.maximum(cs[None, :], ss[:, None]))
    return jnp.asarray(np.clip(ov, 0, None) / NSA_CMP_LEN, dtype=F32)


def _nsa_cmp_attention(nq, kc, vct, batch, seq):
    g = NSA_KV_HEADS
    tq = min(CMP_TILE, seq)
    r = kc.shape[1]
    nsel = seq // NSA_SEL_BLOCK
    gw = NSA_GROUP * HEAD_DIM
    return pl.pallas_call(
        _nsa_cmp_body,
        out_shape=[jax.ShapeDtypeStruct((batch, seq, g * gw), F32),
                   jax.ShapeDtypeStruct((batch, g, seq, LANES), F32)],
        grid=(batch, g, seq // tq),
        in_specs=[pl.BlockSpec((1, tq, gw), lambda b, gi, i: (b, i, gi)),
                  pl.BlockSpec((1, r, LANES), lambda b, gi, i: (b * g + gi, 0, 0)),
                  pl.BlockSpec((1, LANES, r), lambda b, gi, i: (b * g + gi, 0, 0)),
                  _const_spec((nsel, r))],
        out_specs=[pl.BlockSpec((1, tq, gw), lambda b, gi, i: (b, i, gi)),
                   pl.BlockSpec((1, 1, tq, LANES), lambda b, gi, i: (b, gi, i, 0))],
        compiler_params=_cparams(3),
        name="nsa_compressed_attention",
    )(nq, kc, vct, _nsa_overlap_t(r, nsel))


def _nsa_main_body(q_ref, ksa_ref, ksb_ref, vsa_ref, vsb_ref, kw_ref, vwa_ref, vwb_ref,
                   bias_ref, gt_ref, oc_ref, o_ref, acc_ref):
    gi = pl.program_id(1)
    qi = pl.program_id(2)
    tq = q_ref.shape[1]
    kt = min(K_TILE, ksa_ref.shape[2])
    lane = _lane_iota(tq)
    lo = lane < HEAD_DIM
    bias = bias_ref[0, 0]
    q_sel, q_win = [], []
    for hd in range(NSA_GROUP):
        qs = q_ref[0, :, (hd // 2) * LANES:(hd // 2 + 1) * LANES] * QK_SCALE
        half = lo if hd % 2 == 0 else ~lo
        q_sel.append(jnp.where(half, qs, bias).astype(BF16))
        q_win.append(jnp.where(half, qs, 0.0).astype(BF16))
    k_sel = [r.at[0] for r in (ksa_ref, ksb_ref)]
    v_sel = [r.at[0] for r in (vsa_ref, vsb_ref)]
    v_win = [r.at[0] for r in (vwa_ref, vwb_ref)]

    def tile(j, ms, width, mask, rows=slice(None), align=None, *, qs, ks, vs):
        keys = _key_rows(j, width, align)
        s = [lax.dot_general(qs[hd][rows], ks[hd % 2][0, keys, :], NT, preferred_element_type=F32)
             for hd in range(NSA_GROUP)]
        out = []
        for hd in range(NSA_GROUP):
            m, alpha, p = _softmax_step(ms[hd][rows], s[hd], mask)
            _accumulate(acc_ref.at[hd, rows], alpha, p, vs[hd % 2][0, keys, :])
            out.append(m)
        return tuple(out)

    m0 = (jnp.full((tq, 1), M_INIT, F32),) * NSA_GROUP

    acc_ref[...] = jnp.zeros_like(acc_ref)
    _causal_sweep(qi, tq, kt, functools.partial(tile, qs=q_sel, ks=k_sel, vs=v_sel), m0)
    o_sel = [_normalized(acc_ref[hd]) for hd in range(NSA_GROUP)]

    acc_ref[...] = jnp.zeros_like(acc_ref)
    k_win = [kw_ref.at[0]] * 2
    span = NSA_WINDOW + tq
    first = jnp.maximum(qi * tq - NSA_WINDOW, 0)
    delta = (qi * tq - first + lax.broadcasted_iota(jnp.int32, (tq, span), 0)
             - lax.broadcasted_iota(jnp.int32, (tq, span), 1))
    tile(first // tq, m0, span, _mask_bias((delta >= 0) & (delta < NSA_WINDOW)), align=tq,
         qs=q_win, ks=k_win, vs=v_win)
    o_win = [_normalized(acc_ref[hd]) for hd in range(NSA_GROUP)]

    gt = gt_ref[0]

    def gate(branch, hd):
        c = branch * NSA_HEADS + gi * NSA_GROUP + hd
        return jnp.sum(jnp.where(lane == c, gt, 0.0), axis=-1, keepdims=True)

    for pr in range(NSA_GROUP // 2):
        a, b = 2 * pr, 2 * pr + 1
        o = (jnp.where(lo, gate(0, a), gate(0, b)) * oc_ref[0, :, pr * LANES:(pr + 1) * LANES]
             + jnp.where(lo, gate(1, a) * o_sel[a], gate(1, b) * o_sel[b])
             + jnp.where(lo, gate(2, a) * o_win[a], gate(2, b) * o_win[b]))
        o_ref[0, :, pr * LANES:(pr + 1) * LANES] = o.astype(o_ref.dtype)


def _nsa_main_attention(nq, ksa, ksb, vsa, vsb, kw, vwa, vwb, bias, gates, oc):
    batch, seq, w = nq.shape
    g = NSA_KV_HEADS
    tq = min(Q_TILE, seq)
    gw = NSA_GROUP * HEAD_DIM
    kv_spec = pl.BlockSpec((1, 1, seq, LANES), lambda b, gi, i: (b, gi, 0, 0))
    q_spec = pl.BlockSpec((1, tq, gw), lambda b, gi, i: (b, i, gi))
    return pl.pallas_call(
        _nsa_main_body,
        out_shape=jax.ShapeDtypeStruct((batch, seq, w), BF16),
        grid=(batch, g, seq // tq),
        in_specs=[q_spec] + [kv_spec] * 7
                 + [pl.BlockSpec((1, 1, tq, LANES), lambda b, gi, i: (b, gi, i, 0)),
                    pl.BlockSpec((1, tq, LANES), lambda b, gi, i: (b, i, 0)),
                    q_spec],
        out_specs=q_spec,
        scratch_shapes=[pltpu.VMEM((NSA_GROUP, tq, LANES), F32)],
        compiler_params=_cparams(3),
        name="nsa_selected_window_attention",
    )(nq, ksa, ksb, vsa, vsb, kw, vwa, vwb, bias, gates, oc)


def _diff_proj_body(x_ref, an_ref, sc_ref, sh_ref, w_ref, cos_ref, sin_ref, hg_ref,
                    q_ref, k_ref, v_ref):
    h = _norm_mod(x_ref[...], an_ref[...], sc_ref[0], sh_ref[0]).astype(BF16)
    cos = cos_ref[...]
    sin = sin_ref[...]
    w = q_ref.shape[1]
    chunk = 4 * LANES
    for c0 in range(0, w, chunk):
        yq = jnp.dot(h, w_ref[:, c0:c0 + chunk], preferred_element_type=F32)
        yk = jnp.dot(h, w_ref[:, w + c0:w + c0 + chunk], preferred_element_type=F32)
        yv = jnp.dot(h, w_ref[:, 2 * w + c0:2 * w + c0 + chunk], preferred_element_type=F32)
        sq, sk = _head_sumsq(yq), _head_sumsq(yk)
        for p in range(chunk // LANES):
            sl = slice(p * LANES, (p + 1) * LANES)
            dst = slice(c0 + p * LANES, c0 + (p + 1) * LANES)
            q_ref[:, dst] = (_head_norm_rope(yq[:, sl], sq[:, sl], hg_ref[0:1, :], cos, sin) * QK_SCALE).astype(q_ref.dtype)
            k_ref[:, dst] = _head_norm_rope(yk[:, sl], sk[:, sl], hg_ref[1:2, :], cos, sin).astype(k_ref.dtype)
        v_ref[:, c0:c0 + chunk] = yv.astype(v_ref.dtype)


def _diff_proj(x2, an, sc, sh, w, cos, sin, hgains, seq):
    t, d = x2.shape
    rows = min(PROJ_ROWS, seq)
    per_b = seq // rows
    wq = w.shape[1] // 3
    row_spec = lambda n: pl.BlockSpec((rows, n), lambda i: (i, 0))
    mod_spec = pl.BlockSpec((1, 1, d), lambda i: (i // per_b, 0, 0))
    return pl.pallas_call(
        _diff_proj_body,
        out_shape=[jax.ShapeDtypeStruct((t, wq), BF16)] * 3,
        grid=(t // rows,),
        in_specs=[row_spec(d), _const_spec((1, d)), mod_spec, mod_spec, _const_spec(w.shape),
                  row_spec(LANES), row_spec(LANES), _const_spec(hgains.shape)],
        out_specs=[row_spec(wq)] * 3,
        compiler_params=_cparams(1),
        name="diff_in_proj",
    )(x2, an, sc, sh, w, cos, sin, hgains)


def _diff_body(lam_ref, on_ref, q_ref, k_ref, v_ref, o_ref, acc_ref, den_ref, *, lam_init):
    qi = pl.program_id(2)
    tq = q_ref.shape[1]
    kt = min(K_TILE, k_ref.shape[1])
    heads = q_ref.shape[2] // LANES
    lo = _lane_iota(tq) < HEAD_DIM
    qb = []
    for hd in range(heads):
        q = q_ref[0, :, hd * LANES:(hd + 1) * LANES]
        zero = jnp.zeros_like(q)
        qb += [jnp.where(lo, q, zero), jnp.where(lo, zero, q)]
    acc_ref[...] = jnp.zeros_like(acc_ref)
    den_ref[...] = jnp.zeros_like(den_ref)

    def tile(j, ms, width, mask, rows=slice(None), align=None):
        keys = _key_rows(j, width, align)
        k = k_ref[0, keys, :]
        v = v_ref[0, keys, :]
        s = [lax.dot_general(qb[c][rows], k[:, (c // 2) * LANES:(c // 2 + 1) * LANES], NT,
                             preferred_element_type=F32) for c in range(2 * heads)]
        out = []
        for c in range(2 * heads):
            m, alpha, p = _softmax_step(ms[c][rows], s[c], mask)
            parts = [p[:, g * LANES:(g + 1) * LANES] for g in range(width // LANES)]
            while len(parts) > 1:
                parts = [a + b for a, b in zip(parts[0::2], parts[1::2])]
            den_ref[c, rows] = alpha * den_ref[c, rows] + parts[0].astype(F32)
            _accumulate(acc_ref.at[c, rows], alpha, p, v[:, (c // 2) * LANES:(c // 2 + 1) * LANES])
            out.append(m)
        return tuple(out)

    m0 = jnp.full((tq, 1), M_INIT, F32)
    _causal_sweep(qi, tq, kt, tile, (m0,) * (2 * heads))
    lp = lam_ref[...]
    lam = (jnp.exp(jnp.sum(lp[0:1] * lp[1:2], axis=-1, keepdims=True))
           - jnp.exp(jnp.sum(lp[2:3] * lp[3:4], axis=-1, keepdims=True)) + lam_init)
    for hd in range(heads):
        den = [jnp.sum(den_ref[2 * hd + c], axis=-1, keepdims=True) for c in range(2)]
        o = acc_ref[2 * hd] / den[0] - lam * (acc_ref[2 * hd + 1] / den[1])
        ms = jnp.mean(o * o, axis=-1, keepdims=True)
        o = o * lax.rsqrt(ms + NORM_EPS) * on_ref[...] * (1.0 - lam_init)
        o_ref[0, :, hd * LANES:(hd + 1) * LANES] = o.astype(o_ref.dtype)


def _diff_attention(q, k, v, lam_params, out_norm, lam_init):
    batch, seq, w = q.shape
    tq = min(Q_TILE, seq)
    gw = DIFF_HEADS_PER_STEP * LANES
    kv_spec = pl.BlockSpec((1, seq, gw), lambda b, h, i: (b, 0, h))
    q_spec = pl.BlockSpec((1, tq, gw), lambda b, h, i: (b, i, h))
    return pl.pallas_call(
        functools.partial(_diff_body, lam_init=lam_init),
        out_shape=jax.ShapeDtypeStruct((batch, seq, w), BF16),
        grid=(batch, w // gw, seq // tq),
        in_specs=[_const_spec(lam_params.shape), _const_spec((1, LANES)), q_spec, kv_spec, kv_spec],
        out_specs=q_spec,
        scratch_shapes=[pltpu.VMEM((2 * DIFF_HEADS_PER_STEP, tq, LANES), F32)] * 2,
        compiler_params=_cparams(3),
        name="diff_attention",
    )(lam_params, out_norm.reshape(1, LANES), q, k, v)


def _out_ffn_body(*refs, n_attn):
    x_ref = refs[0]
    attn = refs[1:1 + 2 * n_attn]
    ga_ref, fn_ref, sc_ref, sh_ref, gf_ref, wg_ref, wu_ref, wd_ref, o_ref = refs[1 + 2 * n_attn:]
    y = None
    for a in range(n_attn):
        part = jnp.dot(attn[2 * a][...], attn[2 * a + 1][...], preferred_element_type=F32)
        y = part if y is None else y + part
    x1 = x_ref[...] + ga_ref[0] * y
    h = _norm_mod(x1, fn_ref[...], sc_ref[0], sh_ref[0]).astype(BF16)
    acc = jnp.zeros(x1.shape, F32)
    for c0 in range(0, wg_ref.shape[1], FFN_CHUNK):
        gch = jnp.dot(h, wg_ref[:, c0:c0 + FFN_CHUNK], preferred_element_type=F32)
        uch = jnp.dot(h, wu_ref[:, c0:c0 + FFN_CHUNK], preferred_element_type=F32)
        act = (_silu(gch) * uch).astype(BF16)
        acc = acc + jnp.dot(act, wd_ref[c0:c0 + FFN_CHUNK, :], preferred_element_type=F32)
    o_ref[...] = x1 + gf_ref[0] * acc


def _out_ffn(x2, attn_pairs, ga, fn, sc, sh, gf, wg, wu, wd, seq):
    t, d = x2.shape
    rows = min(PROJ_ROWS, seq)
    per_b = seq // rows
    row_spec = lambda n: pl.BlockSpec((rows, n), lambda i: (i, 0))
    mod_spec = pl.BlockSpec((1, 1, d), lambda i: (i // per_b, 0, 0))
    resident = lambda a: pl.BlockSpec(a.shape, lambda i: (0, 0), pipeline_mode=pl.Buffered(1))
    args, specs = [x2], [row_spec(d)]
    for o, w in attn_pairs:
        args += [o, w]
        specs += [row_spec(o.shape[1]), resident(w)]
    args += [ga, fn, sc, sh, gf, wg, wu, wd]
    specs += [mod_spec, _const_spec((1, d)), mod_spec, mod_spec, mod_spec,
              resident(wg), resident(wu), resident(wd)]
    return pl.pallas_call(
        functools.partial(_out_ffn_body, n_attn=len(attn_pairs)),
        out_shape=jax.ShapeDtypeStruct((t, d), F32),
        grid=(t // rows,),
        in_specs=specs,
        out_specs=row_spec(d),
        compiler_params=_cparams(1),
        name="out_proj_swiglu",
    )(*args)


def _rope_lane_tables(positions):
    inv_freq = 1.0 / (ROPE_THETA ** (jnp.arange(0, ROT_DIM, 2, dtype=F32) / ROT_DIM))
    lane = np.arange(LANES) % HEAD_DIM
    rotary = lane < ROT_DIM
    freq = jnp.where(rotary, inv_freq[lane % ROT_HALF], 0.0)
    sign = jnp.asarray(np.where(rotary, np.where(lane < ROT_HALF, -1.0, 1.0), 0.0), F32)
    ang = positions.astype(F32).reshape(-1, 1) * freq
    return jnp.where(rotary, jnp.cos(ang), 1.0), jnp.sin(ang) * sign


def _pair_gain(gain):
    return jnp.concatenate([gain, gain]).reshape(1, LANES)


def _pad_rows(a, rows=8):
    return jnp.concatenate([a, jnp.zeros((rows - a.shape[0], a.shape[1]), a.dtype)], axis=0)


def kernel(x, c, positions, ada_w, ada_b, attn_norm, ffn_norm, ffn_w_gate, ffn_w_up, ffn_w_down, sp_w_in, sp_w_out, moba_q_norm, moba_k_norm, nsa_q_norm, nsa_k_norm, nsa_cmp_pos, nsa_cmp_w1, nsa_cmp_w2, diff_w_in, diff_w_out, diff_q_norm, diff_k_norm, diff_lambda, diff_out_norm):
    batch, seq, d = x.shape
    depth = ada_w.shape[0]
    assert seq % K_TILE == 0 and K_TILE % Q_TILE == 0 and Q_TILE == TAIL_TILE
    assert seq % MOBA_BLOCK == 0 and NSA_WINDOW % Q_TILE == 0 and NSA_WINDOW + Q_TILE <= seq
    assert (seq // MOBA_BLOCK) % SUBLANES == 0 and (seq // NSA_SEL_BLOCK) % SUBLANES == 0
    assert seq // MOBA_BLOCK <= HEAD_DIM and seq // NSA_SEL_BLOCK <= HEAD_DIM
    g = NSA_KV_HEADS
    cos, sin = _rope_lane_tables(positions)
    mod = _modulation(c, ada_w, ada_b).reshape(depth, batch, 6, 1, d)
    x2 = x.reshape(batch * seq, d)
    for i in range(depth):
        sh_a, sc_a, g_a, sh_f, sc_f, g_f = (mod[i, :, n] for n in range(6))
        an = attn_norm[i].reshape(1, d)
        j = i // 2
        to3 = lambda a: a.reshape(batch, seq, a.shape[-1])
        if i % 2 == 0:
            w_in = sp_w_in[j]
            pad = (-w_in.shape[1]) % LANES
            w_pad = jnp.concatenate([w_in, jnp.zeros((d, pad), w_in.dtype)], axis=1).astype(BF16)
            hgains = _pad_rows(jnp.concatenate(
                [_pair_gain(moba_q_norm[j]), _pair_gain(moba_k_norm[j]), _pair_gain(nsa_q_norm[j])]
                + [_pair_gain(nsa_k_norm[j, n]) for n in range(3)], axis=0))
            (mq, mka, mkb, mva, mvb, km, nq, ck, cv, ksa, ksb, vsa, vsb, kw, vwa, vwb, gates) = _sparse_proj(
                x2, an, sc_a, sh_a, w_pad, cos, sin, hgains, batch, seq)
            w512 = MOBA_HEADS * HEAD_DIM
            o_moba = _moba_attention(to3(mq), to3(mka), to3(mkb), to3(mva), to3(mvb),
                                     km.reshape(batch, seq // MOBA_BLOCK, w512))
            r = seq // NSA_CMP_STRIDE
            cw = NSA_CMP_STRIDE * HEAD_DIM
            kc = _compress(ck.reshape(batch * g, r, cw), nsa_cmp_pos[j, 0], nsa_cmp_w1[j, 0], nsa_cmp_w2[j, 0], False)
            vct = _compress(cv.reshape(batch * g, r, cw), nsa_cmp_pos[j, 1], nsa_cmp_w1[j, 1], nsa_cmp_w2[j, 1], True)
            oc, bias = _nsa_cmp_attention(to3(nq), kc, vct, batch, seq)
            o_nsa = _nsa_main_attention(to3(nq), ksa, ksb, vsa, vsb, kw, vwa, vwb, bias, to3(gates), oc)
            w_out = sp_w_out[j].astype(BF16)
            attn_pairs = [(o_moba.reshape(batch * seq, w512), w_out[:w512]),
                          (o_nsa.reshape(batch * seq, w512), w_out[w512:])]
        else:
            lam_init = 0.8 - 0.6 * float(np.exp(-0.3 * i))
            hgains = _pad_rows(jnp.concatenate(
                [_pair_gain(diff_q_norm[j]), _pair_gain(diff_k_norm[j])], axis=0))
            q, k, v = _diff_proj(x2, an, sc_a, sh_a, diff_w_in[j].astype(BF16), cos, sin, hgains, seq)
            o = _diff_attention(to3(q), to3(k), to3(v), diff_lambda[j], diff_out_norm[j], lam_init)
            attn_pairs = [(o.reshape(batch * seq, o.shape[-1]), diff_w_out[j].astype(BF16))]
        x2 = _out_ffn(x2, attn_pairs, g_a, ffn_norm[i].reshape(1, d), sc_f, sh_f, g_f,
                      ffn_w_gate[i].astype(BF16), ffn_w_up[i].astype(BF16),
                      ffn_w_down[i].astype(BF16), seq)
    return x2.reshape(batch, seq, d)
```

```python
import functools

import numpy as np
import jax
import jax.numpy as jnp
from jax import lax
from jax.experimental import pallas as pl
from jax.experimental.pallas import tpu as pltpu

F32 = jnp.float32
BF16 = jnp.bfloat16
HI = lax.Precision.HIGHEST

LANES = 128
HEAD_DIM = 64
ROT_DIM = HEAD_DIM // 4
ROT_HALF = ROT_DIM // 2
ROPE_THETA = 500000.0
NORM_EPS = 1e-6
NEG_INF = -1e30
M_INIT = -1e20
SCALE = HEAD_DIM ** -0.5
QK_SCALE = SCALE * 1.4426950408889634

MOBA_HEADS = 8
MOBA_BLOCK = 256
MOBA_TOPK = 3

NSA_HEADS = 8
NSA_KV_HEADS = 2
NSA_GROUP = NSA_HEADS // NSA_KV_HEADS
NSA_CMP_LEN = 32
NSA_CMP_STRIDE = 16
NSA_SEL_BLOCK = 64
NSA_SEL_TOPK = 16
NSA_WINDOW = 512
NSA_FORCE = 1e4

DIFF_HEADS = 8
DIFF_HEADS_PER_STEP = 2
MOBA_PAIRS_PER_STEP = 4

Q_TILE = 512
K_TILE = 1024
TAIL_TILE = 512
CMP_TILE = 1024
SUBLANES = 8
PROJ_ROWS = 512
FFN_CHUNK = 256
VMEM_LIMIT = 56 * 2 ** 20

NT = (((1,), (1,)), ((), ()))


def _cparams(n_axes):
    return pltpu.CompilerParams(dimension_semantics=("arbitrary",) * n_axes,
                                vmem_limit_bytes=VMEM_LIMIT)


def _const_spec(shape):
    return pl.BlockSpec(shape, lambda *_: (0,) * len(shape))


def _silu(t):
    return t / (1.0 + jnp.exp(-t))


def _sigmoid(t):
    return 1.0 / (1.0 + jnp.exp(-t))


def _lane_iota(rows):
    return lax.broadcasted_iota(jnp.int32, (rows, LANES), 1)


def _mod_body(c_ref, w_ref, b_ref, o_ref):
    a = _silu(c_ref[...])
    o_ref[0] = jnp.dot(a, w_ref[0], precision=HI, preferred_element_type=F32) + b_ref[0]


def _modulation(c, ada_w, ada_b):
    depth, d, n = ada_w.shape
    b = c.shape[0]
    tn = n // 4
    return pl.pallas_call(
        _mod_body,
        out_shape=jax.ShapeDtypeStruct((depth, b, n), F32),
        grid=(depth, n // tn),
        in_specs=[pl.BlockSpec((b, d), lambda i, j: (0, 0)),
                  pl.BlockSpec((1, d, tn), lambda i, j: (i, 0, j)),
                  pl.BlockSpec((1, 1, tn), lambda i, j: (i, 0, j))],
        out_specs=pl.BlockSpec((1, b, tn), lambda i, j: (i, 0, j)),
        compiler_params=_cparams(2),
        name="adaln_modulation",
    )(c, ada_w, ada_b.reshape(depth, 1, n))


def _norm_mod(x, gain, sc, sh):
    ms = jnp.mean(x * x, axis=-1, keepdims=True)
    return x * lax.rsqrt(ms + NORM_EPS) * gain * (1.0 + sc) + sh


def _head_sumsq(y):
    wide = 2 * LANES
    same_head = (lax.broadcasted_iota(jnp.int32, (wide, wide), 0) // HEAD_DIM
                 == lax.broadcasted_iota(jnp.int32, (wide, wide), 1) // HEAD_DIM)
    ones_bd = jnp.where(same_head, 1.0, 0.0).astype(BF16)
    y2 = (y * y).astype(BF16)
    out = [jnp.dot(y2[:, c:c + wide], ones_bd, preferred_element_type=F32)
           for c in range(0, y.shape[1], wide)]
    return out[0] if len(out) == 1 else jnp.concatenate(out, axis=1)


def _head_norm_rope(y, ss, gain, cos, sin):
    rows = y.shape[0]
    lane = _lane_iota(rows)
    yn = y * lax.rsqrt(ss * (1.0 / HEAD_DIM) + NORM_EPS) * gain
    ahead = pltpu.roll(yn, LANES - ROT_HALF, 1)
    behind = pltpu.roll(yn, ROT_HALF, 1)
    partner = jnp.where((lane & (HEAD_DIM - 1)) < ROT_HALF, ahead, behind)
    return yn * cos + partner * sin


def _split_bf16(a, terms):
    out = []
    for _ in range(terms):
        t = a.astype(BF16)
        out.append(t)
        a = a - t.astype(F32)
    return out


def _dot_nt_3pass(a_terms, b):
    a_hi, a_lo = a_terms
    b_hi, b_lo = _split_bf16(b, 2)
    d = lambda u, v: lax.dot_general(u, v, NT, preferred_element_type=F32)
    return d(a_hi, b_hi) + (d(a_hi, b_lo) + d(a_lo, b_hi))


def _dot_3pass(a, b):
    a_hi, a_lo = _split_bf16(a, 2)
    b_hi, b_lo = _split_bf16(b, 2)
    d = lambda u, v: jnp.dot(u, v, preferred_element_type=F32)
    return d(a_hi, b_hi) + (d(a_hi, b_lo) + d(a_lo, b_hi))


def _softmax_step(m, s, bias):
    sb = s.astype(BF16)
    if bias is not None:
        sb = sb + bias
    m_new = jnp.maximum(m, jnp.max(sb, axis=-1, keepdims=True).astype(F32))
    return m_new, jnp.exp2(m - m_new), jnp.exp2(sb - m_new.astype(BF16))


def _mask_bias(mask):
    return jnp.where(mask, 0.0, NEG_INF).astype(BF16)


def _accumulate(acc_ref, alpha, p, v):
    acc_ref[...] = alpha * acc_ref[...] + jnp.dot(p, v, preferred_element_type=F32)


def _normalized(acc):
    return acc / pltpu.roll(acc, HEAD_DIM, 1)


def _topk_rows(score, k, group=None, live_groups=None):
    n, cols = score.shape
    nblk = n // SUBLANES
    blocks = [score[b * SUBLANES:(b + 1) * SUBLANES] for b in range(nblk)]
    sub = lax.broadcasted_iota(jnp.int32, (SUBLANES, cols), 0)

    def count(ranks, lo_row, hi_row):
        ranks = list(ranks)
        for i in range(lo_row, hi_row):
            si = score[i:i + 1, :]
            for b in range(nblk):
                if b > i // SUBLANES:
                    beats = si >= blocks[b]
                elif b < i // SUBLANES:
                    beats = si > blocks[b]
                else:
                    beats = (si > blocks[b]) | ((si == blocks[b]) & (sub > i % SUBLANES))
                ranks[b] = ranks[b] + jnp.where(beats, 1, 0)
        return tuple(ranks)

    ranks = tuple(jnp.zeros((SUBLANES, cols), jnp.int32) for _ in range(nblk))
    if group is None:
        ranks = count(ranks, 0, n)
    else:
        for g in range(n // group):
            ranks = lax.cond(g < live_groups, functools.partial(count, lo_row=g * group, hi_row=(g + 1) * group),
                             lambda r: r, ranks)
    return jnp.concatenate(ranks, axis=0) < k


def _lane_spread(allowed):
    n, cols = allowed.shape
    parts = [allowed, allowed]
    if n < HEAD_DIM:
        fill = jnp.zeros((HEAD_DIM - n, cols), F32)
        parts = [allowed, fill, allowed, fill]
    return jnp.where(jnp.concatenate(parts, axis=0).T > 0.5, 0.0, NEG_INF)


def _row_tile(ref, j, rows):
    return ref[0, pl.ds(pl.multiple_of(j * rows, rows), rows), :]


def _key_rows(j, width, align):
    align = align or width
    return pl.ds(pl.multiple_of(j * align, align), width)


def _causal_sweep(qi, tq, kt, tile, state):
    n_wide = (qi * tq) // kt
    state = lax.fori_loop(0, n_wide, lambda j, c: tile(j, c, kt, None), state)
    state = lax.fori_loop(n_wide * (kt // tq), qi, lambda j, c: tile(j, c, tq, None), state)
    half = tq // 2
    for hh in range(2):
        width = (hh + 1) * half
        row = hh * half + lax.broadcasted_iota(jnp.int32, (half, width), 0)
        col = lax.broadcasted_iota(jnp.int32, (half, width), 1)
        tile(qi, state, width, _mask_bias(col <= row), rows=slice(hh * half, (hh + 1) * half), align=tq)


def _sparse_proj_body(x_ref, an_ref, sc_ref, sh_ref, w_ref, cos_ref, sin_ref, hg_ref,
                      mq_ref, mka_ref, mkb_ref, mva_ref, mvb_ref, km_ref, nq_ref, ck_ref, cv_ref,
                      ksa_ref, ksb_ref, vsa_ref, vsb_ref, kw_ref, vwa_ref, vwb_ref, gt_ref,
                      *, per_b):
    rows = x_ref.shape[0]
    h = _norm_mod(x_ref[...], an_ref[...], sc_ref[0], sh_ref[0]).astype(BF16)
    cos = cos_ref[...]
    sin = sin_ref[...]
    lane = _lane_iota(rows)
    lo = lane < HEAD_DIM
    seq_pos = (pl.program_id(0) % per_b) * rows + lax.broadcasted_iota(jnp.int32, (rows, LANES), 0)
    lane_blk = lane & (HEAD_DIM - 1)
    moba_hot = jnp.where(lane_blk == seq_pos // MOBA_BLOCK, 1.0, 0.0)
    sel_hot = jnp.where(lane_blk == seq_pos // NSA_SEL_BLOCK, 1.0, 0.0)

    def proj(c0, n):
        return jnp.dot(h, w_ref[:, c0:c0 + n], preferred_element_type=F32)

    def group(y, p):
        return y[:, p * LANES:(p + 1) * LANES]

    def nr(y, ss, p, gi):
        return _head_norm_rope(group(y, p), group(ss, p), hg_ref[gi:gi + 1, :], cos, sin)

    def split_groups(y, fill, a_ref, b_ref):
        rolled = pltpu.roll(y, HEAD_DIM, 1)
        a_ref[0, 0] = jnp.where(lo, y, fill).astype(a_ref.dtype)
        b_ref[0, 0] = jnp.where(lo, fill, rolled).astype(b_ref.dtype)
        a_ref[0, 1] = jnp.where(lo, rolled, fill).astype(a_ref.dtype)
        b_ref[0, 1] = jnp.where(lo, fill, y).astype(b_ref.dtype)

    moba_w = MOBA_HEADS * HEAD_DIM
    pairs = moba_w // LANES
    y = proj(0, moba_w)
    ss = _head_sumsq(y)
    for p in range(pairs):
        mq_ref[:, p * LANES:(p + 1) * LANES] = nr(y, ss, p, 0)
    y = proj(moba_w, moba_w)
    ss = _head_sumsq(y)
    for p in range(pairs):
        sl = slice(p * LANES, (p + 1) * LANES)
        k = nr(y, ss, p, 1)
        mka_ref[:, sl] = jnp.where(lo, k, moba_hot).astype(mka_ref.dtype)
        mkb_ref[:, sl] = jnp.where(lo, moba_hot, k).astype(mkb_ref.dtype)
        for r in range(rows // MOBA_BLOCK):
            km_ref[r, :, sl] = jnp.mean(k[r * MOBA_BLOCK:(r + 1) * MOBA_BLOCK], axis=0, keepdims=True)
    y = proj(2 * moba_w, moba_w)
    for p in range(pairs):
        sl = slice(p * LANES, (p + 1) * LANES)
        mva_ref[:, sl] = jnp.where(lo, group(y, p), 1.0).astype(mva_ref.dtype)
        mvb_ref[:, sl] = jnp.where(lo, 1.0, group(y, p)).astype(mvb_ref.dtype)
    c0 = 3 * moba_w
    y = proj(c0, NSA_HEADS * HEAD_DIM)
    ss = _head_sumsq(y)
    for p in range(NSA_HEADS * HEAD_DIM // LANES):
        nq_ref[:, p * LANES:(p + 1) * LANES] = nr(y, ss, p, 2)
    c0 += NSA_HEADS * HEAD_DIM
    y = proj(c0, 6 * LANES)
    ss = _head_sumsq(y)
    kc = nr(y, ss, 0, 3)
    ck_ref[0, 0] = kc[:, :HEAD_DIM]
    ck_ref[0, 1] = kc[:, HEAD_DIM:]
    vc = group(y, 1)
    cv_ref[0, 0] = vc[:, :HEAD_DIM]
    cv_ref[0, 1] = vc[:, HEAD_DIM:]
    split_groups(nr(y, ss, 2, 4), sel_hot, ksa_ref, ksb_ref)
    split_groups(group(y, 3), 1.0, vsa_ref, vsb_ref)
    kw = nr(y, ss, 4, 5)
    kw_rolled = pltpu.roll(kw, HEAD_DIM, 1)
    kw_ref[0, 0] = jnp.where(lo, kw, kw_rolled).astype(kw_ref.dtype)
    kw_ref[0, 1] = jnp.where(lo, kw_rolled, kw).astype(kw_ref.dtype)
    split_groups(group(y, 5), 1.0, vwa_ref, vwb_ref)
    c0 += 6 * LANES
    gt_ref[...] = _sigmoid(proj(c0, LANES))


def _sparse_proj(x2, an, sc, sh, w_pad, cos, sin, hgains, batch, seq):
    t, d = x2.shape
    rows = min(PROJ_ROWS, seq)
    per_b = seq // rows
    nb = seq // MOBA_BLOCK
    g = NSA_KV_HEADS
    w512 = MOBA_HEADS * HEAD_DIM
    row_spec = lambda w: pl.BlockSpec((rows, w), lambda i: (i, 0))
    mod_spec = pl.BlockSpec((1, 1, d), lambda i: (i // per_b, 0, 0))
    grp64 = pl.BlockSpec((1, g, rows, HEAD_DIM), lambda i: (i // per_b, 0, i % per_b, 0))
    grp128 = pl.BlockSpec((1, g, rows, LANES), lambda i: (i // per_b, 0, i % per_b, 0))
    tok = lambda dt: jax.ShapeDtypeStruct((t, w512), dt)
    kv = lambda dt, w: jax.ShapeDtypeStruct((batch, g, seq, w), dt)
    out_shape = [tok(F32), tok(BF16), tok(BF16), tok(BF16), tok(BF16),
                 jax.ShapeDtypeStruct((batch * nb, 1, w512), F32),
                 tok(F32),
                 kv(F32, HEAD_DIM), kv(F32, HEAD_DIM),
                 kv(BF16, LANES), kv(BF16, LANES), kv(BF16, LANES), kv(BF16, LANES),
                 kv(BF16, LANES), kv(BF16, LANES), kv(BF16, LANES),
                 jax.ShapeDtypeStruct((t, LANES), F32)]
    out_specs = ([row_spec(w512)] * 5
                 + [pl.BlockSpec((rows // MOBA_BLOCK, 1, w512), lambda i: (i, 0, 0)), row_spec(w512),
                    grp64, grp64] + [grp128] * 7 + [row_spec(LANES)])
    return pl.pallas_call(
        functools.partial(_sparse_proj_body, per_b=per_b),
        out_shape=out_shape,
        grid=(t // rows,),
        in_specs=[row_spec(d), _const_spec((1, d)), mod_spec, mod_spec,
                  _const_spec(w_pad.shape), row_spec(LANES), row_spec(LANES),
                  _const_spec(hgains.shape)],
        out_specs=out_specs,
        compiler_params=_cparams(1),
        name="sparse_in_proj",
    )(x2, an, sc, sh, w_pad, cos, sin, hgains)


def _moba_body(q_ref, ka_ref, kb_ref, va_ref, vb_ref, km_ref, o_ref, acc_ref):
    qi = pl.program_id(2)
    tq = q_ref.shape[1]
    kt = min(K_TILE, ka_ref.shape[1])
    nb = km_ref.shape[1]
    pairs = q_ref.shape[2] // LANES
    lo = _lane_iota(tq) < HEAD_DIM
    blk = lax.broadcasted_iota(jnp.int32, (nb, tq), 0)
    own = (qi * tq + lax.broadcasted_iota(jnp.int32, (nb, tq), 1)) // MOBA_BLOCK
    past = blk < own
    qp = []
    for pr in range(pairs):
        sl = slice(pr * LANES, (pr + 1) * LANES)
        q = q_ref[0, :, sl]
        km = km_ref[0, :, sl]
        for half in (lo, ~lo):
            gate = lax.dot_general(km, jnp.where(half, q, 0.0), NT, precision=HI, preferred_element_type=F32)
            top = _topk_rows(jnp.where(past, gate, NEG_INF), min(MOBA_TOPK, nb)) & past
            bias = _lane_spread(jnp.where(top | (blk == own), 1.0, 0.0))
            qp.append(jnp.where(half, q * QK_SCALE, bias).astype(BF16))
    acc_ref[...] = jnp.zeros_like(acc_ref)
    k_refs = (ka_ref, kb_ref)
    v_refs = (va_ref, vb_ref)

    def tile(j, ms, width, mask, rows=slice(None), align=None):
        keys = _key_rows(j, width, align)
        ks = [r[0, keys, :] for r in k_refs]
        vs = [r[0, keys, :] for r in v_refs]
        lanes = lambda a, c: a[:, (c // 2) * LANES:(c // 2 + 1) * LANES]
        s = [lax.dot_general(qp[c][rows], lanes(ks[c % 2], c), NT, preferred_element_type=F32)
             for c in range(2 * pairs)]
        out = []
        for c in range(2 * pairs):
            m, alpha, p = _softmax_step(ms[c][rows], s[c], mask)
            _accumulate(acc_ref.at[c, rows], alpha, p, lanes(vs[c % 2], c))
            out.append(m)
        return tuple(out)

    m0 = jnp.full((tq, 1), M_INIT, F32)
    _causal_sweep(qi, tq, kt, tile, (m0,) * (2 * pairs))
    for pr in range(pairs):
        o = jnp.where(lo, _normalized(acc_ref[2 * pr]), _normalized(acc_ref[2 * pr + 1]))
        o_ref[0, :, pr * LANES:(pr + 1) * LANES] = o.astype(o_ref.dtype)


def _moba_attention(mq, mka, mkb, mva, mvb, km):
    batch, seq, w = mq.shape
    tq = min(Q_TILE, seq)
    nb = seq // MOBA_BLOCK
    gw = MOBA_PAIRS_PER_STEP * LANES
    q_spec = pl.BlockSpec((1, tq, gw), lambda b, p, i: (b, i, p))
    kv_spec = pl.BlockSpec((1, seq, gw), lambda b, p, i: (b, 0, p))
    return pl.pallas_call(
        _moba_body,
        out_shape=jax.ShapeDtypeStruct((batch, seq, w), BF16),
        grid=(batch, w // gw, seq // tq),
        in_specs=[q_spec, kv_spec, kv_spec, kv_spec, kv_spec,
                  pl.BlockSpec((1, nb, gw), lambda b, p, i: (b, 0, p))],
        out_specs=q_spec,
        scratch_shapes=[pltpu.VMEM((2 * MOBA_PAIRS_PER_STEP, tq, LANES), F32)],
        compiler_params=_cparams(3),
        name="moba_attention",
    )(mq, mka, mkb, mva, mvb, km)


def _compress_body(t_ref, pos_ref, w1a_ref, w1b_ref, w2_ref, o_ref, *, transposed):
    t = t_ref[0]
    r = t.shape[0]
    a = _dot_3pass(t + pos_ref[0:1, :], w1a_ref[...])
    b = _dot_3pass(t + pos_ref[1:2, :], w1b_ref[...])
    hid = _silu(a + pltpu.roll(b, r - 1, 0))
    out = _dot_3pass(hid, w2_ref[...])
    rowi = lax.broadcasted_iota(jnp.int32, out.shape, 0)
    out = jnp.where(rowi < r - 1, out, 0.0)
    o_ref[0] = out.T if transposed else out


def _compress(t, pos, w1, w2, transposed):
    n, r, w = t.shape
    out_dims = (LANES, r) if transposed else (r, LANES)
    half = NSA_CMP_LEN // 2
    hidden = w1.shape[-1]
    w1a = w1[:half].reshape(w, hidden)
    w1b = w1[half:].reshape(w, hidden)
    w2d = jnp.concatenate([w2, w2], axis=1)
    return pl.pallas_call(
        functools.partial(_compress_body, transposed=transposed),
        out_shape=jax.ShapeDtypeStruct((n,) + out_dims, F32),
        grid=(n,),
        in_specs=[pl.BlockSpec((1, r, w), lambda i: (i, 0, 0)), _const_spec((2, w)),
                  _const_spec((w, hidden)), _const_spec((w, hidden)), _const_spec((hidden, LANES))],
        out_specs=pl.BlockSpec((1,) + out_dims, lambda i: (i, 0, 0)),
        compiler_params=_cparams(1),
        name="nsa_compress",
    )(t, pos.reshape(2, w), w1a, w1b, w2d)


def _nsa_cmp_body(q_ref, kc_ref, vct_ref, ovt_ref, oc_ref, bias_ref):
    qi = pl.program_id(2)
    tq = q_ref.shape[1]
    r = kc_ref.shape[1]
    nsel = ovt_ref.shape[0]
    kc_terms = _split_bf16(kc_ref[0], 2)
    vct_b = vct_ref[0].astype(BF16)
    lo = _lane_iota(tq) < HEAD_DIM
    n = lax.broadcasted_iota(jnp.int32, (r, tq), 0)
    pos = qi * tq + lax.broadcasted_iota(jnp.int32, (r, tq), 1)
    ok = (n * NSA_CMP_STRIDE + (NSA_CMP_LEN - 1) <= pos) & (n < r - 1)
    psum = jnp.zeros((r, tq), F32)
    outs = []
    for hd in range(NSA_GROUP):
        qp = q_ref[0, :, (hd // 2) * LANES:(hd // 2 + 1) * LANES]
        qm = jnp.where(lo if hd % 2 == 0 else ~lo, qp, 0.0)
        s = _dot_nt_3pass(kc_terms, qm) * SCALE
        s = jnp.where(ok, s, NEG_INF)
        e = jnp.where(ok, jnp.exp(s - jnp.max(s, axis=0, keepdims=True)), 0.0)
        l = jnp.sum(e, axis=0, keepdims=True)
        p = e * jnp.where(l > 0.0, 1.0 / l, 0.0)
        psum = psum + p
        outs.append(jnp.dot(vct_b, p.astype(BF16), preferred_element_type=F32))
    even_rows = lax.broadcasted_iota(jnp.int32, (LANES, tq), 0) < HEAD_DIM
    for pr in range(NSA_GROUP // 2):
        oc_ref[0, :, pr * LANES:(pr + 1) * LANES] = jnp.where(even_rows, outs[2 * pr], outs[2 * pr + 1]).T
    ovt = ovt_ref[...].astype(BF16)
    imp = sum(jnp.dot(ovt, t, preferred_element_type=F32) for t in _split_bf16(psum, 3))
    blk = lax.broadcasted_iota(jnp.int32, (nsel, tq), 0)
    cur = (qi * tq + lax.broadcasted_iota(jnp.int32, (nsel, tq), 1)) // NSA_SEL_BLOCK
    ok_blk = blk <= cur
    forced = ok_blk & ((blk == 0) | (blk >= cur - 1))
    score = jnp.where(ok_blk, jnp.where(forced, NSA_FORCE, imp), NEG_INF)
    chosen = _topk_rows(score, min(NSA_SEL_TOPK, nsel), tq // NSA_SEL_BLOCK, qi + 1) & ok_blk
    bias_ref[0, 0] = _lane_spread(jnp.where(chosen, 1.0, 0.0))


def _nsa_overlap_t(r, nsel):
    cs = np.arange(r) * NSA_CMP_STRIDE
    ss = np.arange(nsel) * NSA_SEL_BLOCK
    ov = (np.minimum(cs[None, :] + NSA_CMP_LEN, ss[:, None] + NSA_SEL_BLOCK)
          - np.maximum(cs[None, :], ss[:, None]))
    return jnp.asarray(np.clip(ov, 0, None) / NSA_CMP_LEN, dtype=F32)


def _nsa_cmp_attention(nq, kc, vct, batch, seq):
    g = NSA_KV_HEADS
    tq = min(CMP_TILE, seq)
    r = kc.shape[1]
    nsel = seq // NSA_SEL_BLOCK
    gw = NSA_GROUP * HEAD_DIM
    return pl.pallas_call(
        _nsa_cmp_body,
        out_shape=[jax.ShapeDtypeStruct((batch, seq, g * gw), F32),
                   jax.ShapeDtypeStruct((batch, g, seq, LANES), F32)],
        grid=(batch, g, seq // tq),
        in_specs=[pl.BlockSpec((1, tq, gw), lambda b, gi, i: (b, i, gi)),
                  pl.BlockSpec((1, r, LANES), lambda b, gi, i: (b * g + gi, 0, 0)),
                  pl.BlockSpec((1, LANES, r), lambda b, gi, i: (b * g + gi, 0, 0)),
                  _const_spec((nsel, r))],
        out_specs=[pl.BlockSpec((1, tq, gw), lambda b, gi, i: (b, i, gi)),
                   pl.BlockSpec((1, 1, tq, LANES), lambda b, gi, i: (b, gi, i, 0))],
        compiler_params=_cparams(3),
        name="nsa_compressed_attention",
    )(nq, kc, vct, _nsa_overlap_t(r, nsel))


def _nsa_main_body(q_ref, ksa_ref, ksb_ref, vsa_ref, vsb_ref, kw_ref, vwa_ref, vwb_ref,
                   bias_ref, gt_ref, oc_ref, o_ref, acc_ref):
    gi = pl.program_id(1)
    qi = pl.program_id(2)
    tq = q_ref.shape[1]
    kt = min(K_TILE, ksa_ref.shape[2])
    lane = _lane_iota(tq)
    lo = lane < HEAD_DIM
    bias = bias_ref[0, 0]
    q_sel, q_win = [], []
    for hd in range(NSA_GROUP):
        qs = q_ref[0, :, (hd // 2) * LANES:(hd // 2 + 1) * LANES] * QK_SCALE
        half = lo if hd % 2 == 0 else ~lo
        q_sel.append(jnp.where(half, qs, bias).astype(BF16))
        q_win.append(jnp.where(half, qs, 0.0).astype(BF16))
    k_sel = [r.at[0] for r in (ksa_ref, ksb_ref)]
    v_sel = [r.at[0] for r in (vsa_ref, vsb_ref)]
    v_win = [r.at[0] for r in (vwa_ref, vwb_ref)]

    def tile(j, ms, width, mask, rows=slice(None), align=None, *, qs, ks, vs):
        keys = _key_rows(j, width, align)
        s = [lax.dot_general(qs[hd][rows], ks[hd % 2][0, keys, :], NT, preferred_element_type=F32)
             for hd in range(NSA_GROUP)]
        out = []
        for hd in range(NSA_GROUP):
            m, alpha, p = _softmax_step(ms[hd][rows], s[hd], mask)
            _accumulate(acc_ref.at[hd, rows], alpha, p, vs[hd % 2][0, keys, :])
            out.append(m)
        return tuple(out)

    m0 = (jnp.full((tq, 1), M_INIT, F32),) * NSA_GROUP

    acc_ref[...] = jnp.zeros_like(acc_ref)
    _causal_sweep(qi, tq, kt, functools.partial(tile, qs=q_sel, ks=k_sel, vs=v_sel), m0)
    o_sel = [_normalized(acc_ref[hd]) for hd in range(NSA_GROUP)]

    acc_ref[...] = jnp.zeros_like(acc_ref)
    k_win = [kw_ref.at[0]] * 2
    span = NSA_WINDOW + tq
    first = jnp.maximum(qi * tq - NSA_WINDOW, 0)
    delta = (qi * tq - first + lax.broadcasted_iota(jnp.int32, (tq, span), 0)
             - lax.broadcasted_iota(jnp.int32, (tq, span), 1))
    tile(first // tq, m0, span, _mask_bias((delta >= 0) & (delta < NSA_WINDOW)), align=tq,
         qs=q_win, ks=k_win, vs=v_win)
    o_win = [_normalized(acc_ref[hd]) for hd in range(NSA_GROUP)]

    gt = gt_ref[0]

    def gate(branch, hd):
        c = branch * NSA_HEADS + gi * NSA_GROUP + hd
        return jnp.sum(jnp.where(lane == c, gt, 0.0), axis=-1, keepdims=True)

    for pr in range(NSA_GROUP // 2):
        a, b = 2 * pr, 2 * pr + 1
        o = (jnp.where(lo, gate(0, a), gate(0, b)) * oc_ref[0, :, pr * LANES:(pr + 1) * LANES]
             + jnp.where(lo, gate(1, a) * o_sel[a], gate(1, b) * o_sel[b])
             + jnp.where(lo, gate(2, a) * o_win[a], gate(2, b) * o_win[b]))
        o_ref[0, :, pr * LANES:(pr + 1) * LANES] = o.astype(o_ref.dtype)


def _nsa_main_attention(nq, ksa, ksb, vsa, vsb, kw, vwa, vwb, bias, gates, oc):
    batch, seq, w = nq.shape
    g = NSA_KV_HEADS
    tq = min(Q_TILE, seq)
    gw = NSA_GROUP * HEAD_DIM
    kv_spec = pl.BlockSpec((1, 1, seq, LANES), lambda b, gi, i: (b, gi, 0, 0))
    q_spec = pl.BlockSpec((1, tq, gw), lambda b, gi, i: (b, i, gi))
    return pl.pallas_call(
        _nsa_main_body,
        out_shape=jax.ShapeDtypeStruct((batch, seq, w), BF16),
        grid=(batch, g, seq // tq),
        in_specs=[q_spec] + [kv_spec] * 7
                 + [pl.BlockSpec((1, 1, tq, LANES), lambda b, gi, i: (b, gi, i, 0)),
                    pl.BlockSpec((1, tq, LANES), lambda b, gi, i: (b, i, 0)),
                    q_spec],
        out_specs=q_spec,
        scratch_shapes=[pltpu.VMEM((NSA_GROUP, tq, LANES), F32)],
        compiler_params=_cparams(3),
        name="nsa_selected_window_attention",
    )(nq, ksa, ksb, vsa, vsb, kw, vwa, vwb, bias, gates, oc)


def _diff_proj_body(x_ref, an_ref, sc_ref, sh_ref, w_ref, cos_ref, sin_ref, hg_ref,
                    q_ref, k_ref, v_ref):
    h = _norm_mod(x_ref[...], an_ref[...], sc_ref[0], sh_ref[0]).astype(BF16)
    cos = cos_ref[...]
    sin = sin_ref[...]
    w = q_ref.shape[1]
    chunk = 4 * LANES
    for c0 in range(0, w, chunk):
        yq = jnp.dot(h, w_ref[:, c0:c0 + chunk], preferred_element_type=F32)
        yk = jnp.dot(h, w_ref[:, w + c0:w + c0 + chunk], preferred_element_type=F32)
        yv = jnp.dot(h, w_ref[:, 2 * w + c0:2 * w + c0 + chunk], preferred_element_type=F32)
        sq, sk = _head_sumsq(yq), _head_sumsq(yk)
        for p in range(chunk // LANES):
            sl = slice(p * LANES, (p + 1) * LANES)
            dst = slice(c0 + p * LANES, c0 + (p + 1) * LANES)
            q_ref[:, dst] = (_head_norm_rope(yq[:, sl], sq[:, sl], hg_ref[0:1, :], cos, sin) * QK_SCALE).astype(q_ref.dtype)
            k_ref[:, dst] = _head_norm_rope(yk[:, sl], sk[:, sl], hg_ref[1:2, :], cos, sin).astype(k_ref.dtype)
        v_ref[:, c0:c0 + chunk] = yv.astype(v_ref.dtype)


def _diff_proj(x2, an, sc, sh, w, cos, sin, hgains, seq):
    t, d = x2.shape
    rows = min(PROJ_ROWS, seq)
    per_b = seq // rows
    wq = w.shape[1] // 3
    row_spec = lambda n: pl.BlockSpec((rows, n), lambda i: (i, 0))
    mod_spec = pl.BlockSpec((1, 1, d), lambda i: (i // per_b, 0, 0))
    return pl.pallas_call(
        _diff_proj_body,
        out_shape=[jax.ShapeDtypeStruct((t, wq), BF16)] * 3,
        grid=(t // rows,),
        in_specs=[row_spec(d), _const_spec((1, d)), mod_spec, mod_spec, _const_spec(w.shape),
                  row_spec(LANES), row_spec(LANES), _const_spec(hgains.shape)],
        out_specs=[row_spec(wq)] * 3,
        compiler_params=_cparams(1),
        name="diff_in_proj",
    )(x2, an, sc, sh, w, cos, sin, hgains)


def _diff_body(lam_ref, on_ref, q_ref, k_ref, v_ref, o_ref, acc_ref, den_ref, *, lam_init):
    qi = pl.program_id(2)
    tq = q_ref.shape[1]
    kt = min(K_TILE, k_ref.shape[1])
    heads = q_ref.shape[2] // LANES
    lo = _lane_iota(tq) < HEAD_DIM
    qb = []
    for hd in range(heads):
        q = q_ref[0, :, hd * LANES:(hd + 1) * LANES]
        zero = jnp.zeros_like(q)
        qb += [jnp.where(lo, q, zero), jnp.where(lo, zero, q)]
    acc_ref[...] = jnp.zeros_like(acc_ref)
    den_ref[...] = jnp.zeros_like(den_ref)

    def tile(j, ms, width, mask, rows=slice(None), align=None):
        keys = _key_rows(j, width, align)
        k = k_ref[0, keys, :]
        v = v_ref[0, keys, :]
        s = [lax.dot_general(qb[c][rows], k[:, (c // 2) * LANES:(c // 2 + 1) * LANES], NT,
                             preferred_element_type=F32) for c in range(2 * heads)]
        out = []
        for c in range(2 * heads):
            m, alpha, p = _softmax_step(ms[c][rows], s[c], mask)
            parts = [p[:, g * LANES:(g + 1) * LANES] for g in range(width // LANES)]
            while len(parts) > 1:
                parts = [a + b for a, b in zip(parts[0::2], parts[1::2])]
            den_ref[c, rows] = alpha * den_ref[c, rows] + parts[0].astype(F32)
            _accumulate(acc_ref.at[c, rows], alpha, p, v[:, (c // 2) * LANES:(c // 2 + 1) * LANES])
            out.append(m)
        return tuple(out)

    m0 = jnp.full((tq, 1), M_INIT, F32)
    _causal_sweep(qi, tq, kt, tile, (m0,) * (2 * heads))
    lp = lam_ref[...]
    lam = (jnp.exp(jnp.sum(lp[0:1] * lp[1:2], axis=-1, keepdims=True))
           - jnp.exp(jnp.sum(lp[2:3] * lp[3:4], axis=-1, keepdims=True)) + lam_init)
    for hd in range(heads):
        den = [jnp.sum(den_ref[2 * hd + c], axis=-1, keepdims=True) for c in range(2)]
        o = acc_ref[2 * hd] / den[0] - lam * (acc_ref[2 * hd + 1] / den[1])
        ms = jnp.mean(o * o, axis=-1, keepdims=True)
        o = o * lax.rsqrt(ms + NORM_EPS) * on_ref[...] * (1.0 - lam_init)
        o_ref[0, :, hd * LANES:(hd + 1) * LANES] = o.astype(o_ref.dtype)


def _diff_attention(q, k, v, lam_params, out_norm, lam_init):
    batch, seq, w = q.shape
    tq = min(Q_TILE, seq)
    gw = DIFF_HEADS_PER_STEP * LANES
    kv_spec = pl.BlockSpec((1, seq, gw), lambda b, h, i: (b, 0, h))
    q_spec = pl.BlockSpec((1, tq, gw), lambda b, h, i: (b, i, h))
    return pl.pallas_call(
        functools.partial(_diff_body, lam_init=lam_init),
        out_shape=jax.ShapeDtypeStruct((batch, seq, w), BF16),
        grid=(batch, w // gw, seq // tq),
        in_specs=[_const_spec(lam_params.shape), _const_spec((1, LANES)), q_spec, kv_spec, kv_spec],
        out_specs=q_spec,
        scratch_shapes=[pltpu.VMEM((2 * DIFF_HEADS_PER_STEP, tq, LANES), F32)] * 2,
        compiler_params=_cparams(3),
        name="diff_attention",
    )(lam_params, out_norm.reshape(1, LANES), q, k, v)


def _out_ffn_body(*refs, n_attn):
    x_ref = refs[0]
    attn = refs[1:1 + 2 * n_attn]
    ga_ref, fn_ref, sc_ref, sh_ref, gf_ref, wg_ref, wu_ref, wd_ref, o_ref = refs[1 + 2 * n_attn:]
    y = None
    for a in range(n_attn):
        part = jnp.dot(attn[2 * a][...], attn[2 * a + 1][...], preferred_element_type=F32)
        y = part if y is None else y + part
    x1 = x_ref[...] + ga_ref[0] * y
    h = _norm_mod(x1, fn_ref[...], sc_ref[0], sh_ref[0]).astype(BF16)
    acc = jnp.zeros(x1.shape, F32)
    for c0 in range(0, wg_ref.shape[1], FFN_CHUNK):
        gch = jnp.dot(h, wg_ref[:, c0:c0 + FFN_CHUNK], preferred_element_type=F32)
        uch = jnp.dot(h, wu_ref[:, c0:c0 + FFN_CHUNK], preferred_element_type=F32)
        act = (_silu(gch) * uch).astype(BF16)
        acc = acc + jnp.dot(act, wd_ref[c0:c0 + FFN_CHUNK, :], preferred_element_type=F32)
    o_ref[...] = x1 + gf_ref[0] * acc


def _out_ffn(x2, attn_pairs, ga, fn, sc, sh, gf, wg, wu, wd, seq):
    t, d = x2.shape
    rows = min(PROJ_ROWS, seq)
    per_b = seq // rows
    row_spec = lambda n: pl.BlockSpec((rows, n), lambda i: (i, 0))
    mod_spec = pl.BlockSpec((1, 1, d), lambda i: (i // per_b, 0, 0))
    resident = lambda a: pl.BlockSpec(a.shape, lambda i: (0, 0), pipeline_mode=pl.Buffered(1))
    args, specs = [x2], [row_spec(d)]
    for o, w in attn_pairs:
        args += [o, w]
        specs += [row_spec(o.shape[1]), resident(w)]
    args += [ga, fn, sc, sh, gf, wg, wu, wd]
    specs += [mod_spec, _const_spec((1, d)), mod_spec, mod_spec, mod_spec,
              resident(wg), resident(wu), resident(wd)]
    return pl.pallas_call(
        functools.partial(_out_ffn_body, n_attn=len(attn_pairs)),
        out_shape=jax.ShapeDtypeStruct((t, d), F32),
        grid=(t // rows,),
        in_specs=specs,
        out_specs=row_spec(d),
        compiler_params=_cparams(1),
        name="out_proj_swiglu",
    )(*args)


def _rope_lane_tables(positions):
    inv_freq = 1.0 / (ROPE_THETA ** (jnp.arange(0, ROT_DIM, 2, dtype=F32) / ROT_DIM))
    lane = np.arange(LANES) % HEAD_DIM
    rotary = lane < ROT_DIM
    freq = jnp.where(rotary, inv_freq[lane % ROT_HALF], 0.0)
    sign = jnp.asarray(np.where(rotary, np.where(lane < ROT_HALF, -1.0, 1.0), 0.0), F32)
    ang = positions.astype(F32).reshape(-1, 1) * freq
    return jnp.where(rotary, jnp.cos(ang), 1.0), jnp.sin(ang) * sign


def _pair_gain(gain):
    return jnp.concatenate([gain, gain]).reshape(1, LANES)


def _pad_rows(a, rows=8):
    return jnp.concatenate([a, jnp.zeros((rows - a.shape[0], a.shape[1]), a.dtype)], axis=0)


def kernel(x, c, positions, ada_w, ada_b, attn_norm, ffn_norm, ffn_w_gate, ffn_w_up, ffn_w_down, sp_w_in, sp_w_out, moba_q_norm, moba_k_norm, nsa_q_norm, nsa_k_norm, nsa_cmp_pos, nsa_cmp_w1, nsa_cmp_w2, diff_w_in, diff_w_out, diff_q_norm, diff_k_norm, diff_lambda, diff_out_norm):
    batch, seq, d = x.shape
    depth = ada_w.shape[0]
    assert seq % K_TILE == 0 and K_TILE % Q_TILE == 0 and Q_TILE == TAIL_TILE
    assert seq % MOBA_BLOCK == 0 and NSA_WINDOW % Q_TILE == 0 and NSA_WINDOW + Q_TILE <= seq
    assert (seq // MOBA_BLOCK) % SUBLANES == 0 and (seq // NSA_SEL_BLOCK) % SUBLANES == 0
    assert seq // MOBA_BLOCK <= HEAD_DIM and seq // NSA_SEL_BLOCK <= HEAD_DIM
    g = NSA_KV_HEADS
    cos, sin = _rope_lane_tables(positions)
    mod = _modulation(c, ada_w, ada_b).reshape(depth, batch, 6, 1, d)
    x2 = x.reshape(batch * seq, d)
    for i in range(depth):
        sh_a, sc_a, g_a, sh_f, sc_f, g_f = (mod[i, :, n] for n in range(6))
        an = attn_norm[i].reshape(1, d)
        j = i // 2
        to3 = lambda a: a.reshape(batch, seq, a.shape[-1])
        if i % 2 == 0:
            w_in = sp_w_in[j]
            pad = (-w_in.shape[1]) % LANES
            w_pad = jnp.concatenate([w_in, jnp.zeros((d, pad), w_in.dtype)], axis=1).astype(BF16)
            hgains = _pad_rows(jnp.concatenate(
                [_pair_gain(moba_q_norm[j]), _pair_gain(moba_k_norm[j]), _pair_gain(nsa_q_norm[j])]
                + [_pair_gain(nsa_k_norm[j, n]) for n in range(3)], axis=0))
            (mq, mka, mkb, mva, mvb, km, nq, ck, cv, ksa, ksb, vsa, vsb, kw, vwa, vwb, gates) = _sparse_proj(
                x2, an, sc_a, sh_a, w_pad, cos, sin, hgains, batch, seq)
            w512 = MOBA_HEADS * HEAD_DIM
            o_moba = _moba_attention(to3(mq), to3(mka), to3(mkb), to3(mva), to3(mvb),
                                     km.reshape(batch, seq // MOBA_BLOCK, w512))
            r = seq // NSA_CMP_STRIDE
            cw = NSA_CMP_STRIDE * HEAD_DIM
            kc = _compress(ck.reshape(batch * g, r, cw), nsa_cmp_pos[j, 0], nsa_cmp_w1[j, 0], nsa_cmp_w2[j, 0], False)
            vct = _compress(cv.reshape(batch * g, r, cw), nsa_cmp_pos[j, 1], nsa_cmp_w1[j, 1], nsa_cmp_w2[j, 1], True)
            oc, bias = _nsa_cmp_attention(to3(nq), kc, vct, batch, seq)
            o_nsa = _nsa_main_attention(to3(nq), ksa, ksb, vsa, vsb, kw, vwa, vwb, bias, to3(gates), oc)
            w_out = sp_w_out[j].astype(BF16)
            attn_pairs = [(o_moba.reshape(batch * seq, w512), w_out[:w512]),
                          (o_nsa.reshape(batch * seq, w512), w_out[w512:])]
        else:
            lam_init = 0.8 - 0.6 * float(np.exp(-0.3 * i))
            hgains = _pad_rows(jnp.concatenate(
                [_pair_gain(diff_q_norm[j]), _pair_gain(diff_k_norm[j])], axis=0))
            q, k, v = _diff_proj(x2, an, sc_a, sh_a, diff_w_in[j].astype(BF16), cos, sin, hgains, seq)
            o = _diff_attention(to3(q), to3(k), to3(v), diff_lambda[j], diff_out_norm[j], lam_init)
            attn_pairs = [(o.reshape(batch * seq, o.shape[-1]), diff_w_out[j].astype(BF16))]
        x2 = _out_ffn(x2, attn_pairs, g_a, ffn_norm[i].reshape(1, d), sc_f, sh_f, g_f,
                      ffn_w_gate[i].astype(BF16), ffn_w_up[i].astype(BF16),
                      ffn_w_down[i].astype(BF16), seq)
    return x2.reshape(batch, seq, d)
```

```python
import functools

import numpy as np
import jax
import jax.numpy as jnp
from jax import lax
from jax.experimental import pallas as pl
from jax.experimental.pallas import tpu as pltpu

F32 = jnp.float32
BF16 = jnp.bfloat16
HI = lax.Precision.HIGHEST

LANES = 128
HEAD_DIM = 64
ROT_DIM = HEAD_DIM // 4
ROT_HALF = ROT_DIM // 2
ROPE_THETA = 500000.0
NORM_EPS = 1e-6
NEG_INF = -1e30
M_INIT = -1e20
SCALE = HEAD_DIM ** -0.5
QK_SCALE = SCALE * 1.4426950408889634

MOBA_HEADS = 8
MOBA_BLOCK = 256
MOBA_TOPK = 3

NSA_HEADS = 8
NSA_KV_HEADS = 2
NSA_GROUP = NSA_HEADS // NSA_KV_HEADS
NSA_CMP_LEN = 32
NSA_CMP_STRIDE = 16
NSA_SEL_BLOCK = 64
NSA_SEL_TOPK = 16
NSA_WINDOW = 512
NSA_FORCE = 1e4

DIFF_HEADS = 8
DIFF_HEADS_PER_STEP = 2
MOBA_PAIRS_PER_STEP = 4

Q_TILE = 512
K_TILE = 1024
TAIL_TILE = 512
CMP_TILE = 1024
SUBLANES = 8
PROJ_ROWS = 512
FFN_CHUNK = 256
VMEM_LIMIT = 56 * 2 ** 20

NT = (((1,), (1,)), ((), ()))


def _cparams(n_axes):
    return pltpu.CompilerParams(dimension_semantics=("arbitrary",) * n_axes,
                                vmem_limit_bytes=VMEM_LIMIT)


def _const_spec(shape):
    return pl.BlockSpec(shape, lambda *_: (0,) * len(shape))


def _silu(t):
    return t / (1.0 + jnp.exp(-t))


def _sigmoid(t):
    return 1.0 / (1.0 + jnp.exp(-t))


def _lane_iota(rows):
    return lax.broadcasted_iota(jnp.int32, (rows, LANES), 1)


def _mod_body(c_ref, w_ref, b_ref, o_ref):
    a = _silu(c_ref[...])
    o_ref[0] = jnp.dot(a, w_ref[0], precision=HI, preferred_element_type=F32) + b_ref[0]


def _modulation(c, ada_w, ada_b):
    depth, d, n = ada_w.shape
    b = c.shape[0]
    tn = n // 4
    return pl.pallas_call(
        _mod_body,
        out_shape=jax.ShapeDtypeStruct((depth, b, n), F32),
        grid=(depth, n // tn),
        in_specs=[pl.BlockSpec((b, d), lambda i, j: (0, 0)),
                  pl.BlockSpec((1, d, tn), lambda i, j: (i, 0, j)),
                  pl.BlockSpec((1, 1, tn), lambda i, j: (i, 0, j))],
        out_specs=pl.BlockSpec((1, b, tn), lambda i, j: (i, 0, j)),
        compiler_params=_cparams(2),
        name="adaln_modulation",
    )(c, ada_w, ada_b.reshape(depth, 1, n))


def _norm_mod(x, gain, sc, sh):
    ms = jnp.mean(x * x, axis=-1, keepdims=True)
    return x * lax.rsqrt(ms + NORM_EPS) * gain * (1.0 + sc) + sh


def _head_sumsq(y):
    wide = 2 * LANES
    same_head = (lax.broadcasted_iota(jnp.int32, (wide, wide), 0) // HEAD_DIM
                 == lax.broadcasted_iota(jnp.int32, (wide, wide), 1) // HEAD_DIM)
    ones_bd = jnp.where(same_head, 1.0, 0.0).astype(BF16)
    y2 = (y * y).astype(BF16)
    out = [jnp.dot(y2[:, c:c + wide], ones_bd, preferred_element_type=F32)
           for c in range(0, y.shape[1], wide)]
    return out[0] if len(out) == 1 else jnp.concatenate(out, axis=1)


def _head_norm_rope(y, ss, gain, cos, sin):
    rows = y.shape[0]
    lane = _lane_iota(rows)
    yn = y * lax.rsqrt(ss * (1.0 / HEAD_DIM) + NORM_EPS) * gain
    ahead = pltpu.roll(yn, LANES - ROT_HALF, 1)
    behind = pltpu.roll(yn, ROT_HALF, 1)
    partner = jnp.where((lane & (HEAD_DIM - 1)) < ROT_HALF, ahead, behind)
    return yn * cos + partner * sin


def _split_bf16(a, terms):
    out = []
    for _ in range(terms):
        t = a.astype(BF16)
        out.append(t)
        a = a - t.astype(F32)
    return out


def _dot_nt_3pass(a_terms, b):
    a_hi, a_lo = a_terms
    b_hi, b_lo = _split_bf16(b, 2)
    d = lambda u, v: lax.dot_general(u, v, NT, preferred_element_type=F32)
    return d(a_hi, b_hi) + (d(a_hi, b_lo) + d(a_lo, b_hi))


def _dot_3pass(a, b):
    a_hi, a_lo = _split_bf16(a, 2)
    b_hi, b_lo = _split_bf16(b, 2)
    d = lambda u, v: jnp.dot(u, v, preferred_element_type=F32)
    return d(a_hi, b_hi) + (d(a_hi, b_lo) + d(a_lo, b_hi))


def _softmax_step(m, s, bias):
    sb = s.astype(BF16)
    if bias is not None:
        sb = sb + bias
    m_new = jnp.maximum(m, jnp.max(sb, axis=-1, keepdims=True).astype(F32))
    return m_new, jnp.exp2(m - m_new), jnp.exp2(sb - m_new.astype(BF16))


def _mask_bias(mask):
    return jnp.where(mask, 0.0, NEG_INF).astype(BF16)


def _accumulate(acc_ref, alpha, p, v):
    acc_ref[...] = alpha * acc_ref[...] + jnp.dot(p, v, preferred_element_type=F32)


def _normalized(acc):
    return acc / pltpu.roll(acc, HEAD_DIM, 1)


def _topk_rows(score, k, group=None, live_groups=None):
    n, cols = score.shape
    nblk = n // SUBLANES
    blocks = [score[b * SUBLANES:(b + 1) * SUBLANES] for b in range(nblk)]
    sub = lax.broadcasted_iota(jnp.int32, (SUBLANES, cols), 0)

    def count(ranks, lo_row, hi_row):
        ranks = list(ranks)
        for i in range(lo_row, hi_row):
            si = score[i:i + 1, :]
            for b in range(nblk):
                if b > i // SUBLANES:
                    beats = si >= blocks[b]
                elif b < i // SUBLANES:
                    beats = si > blocks[b]
                else:
                    beats = (si > blocks[b]) | ((si == blocks[b]) & (sub > i % SUBLANES))
                ranks[b] = ranks[b] + jnp.where(beats, 1, 0)
        return tuple(ranks)

    ranks = tuple(jnp.zeros((SUBLANES, cols), jnp.int32) for _ in range(nblk))
    if group is None:
        ranks = count(ranks, 0, n)
    else:
        for g in range(n // group):
            ranks = lax.cond(g < live_groups, functools.partial(count, lo_row=g * group, hi_row=(g + 1) * group),
                             lambda r: r, ranks)
    return jnp.concatenate(ranks, axis=0) < k


def _lane_spread(allowed):
    n, cols = allowed.shape
    parts = [allowed, allowed]
    if n < HEAD_DIM:
        fill = jnp.zeros((HEAD_DIM - n, cols), F32)
        parts = [allowed, fill, allowed, fill]
    return jnp.where(jnp.concatenate(parts, axis=0).T > 0.5, 0.0, NEG_INF)


def _row_tile(ref, j, rows):
    return ref[0, pl.ds(pl.multiple_of(j * rows, rows), rows), :]


def _key_rows(j, width, align):
    align = align or width
    return pl.ds(pl.multiple_of(j * align, align), width)


def _causal_sweep(qi, tq, kt, tile, state):
    n_wide = (qi * tq) // kt
    state = lax.fori_loop(0, n_wide, lambda j, c: tile(j, c, kt, None), state)
    state = lax.fori_loop(n_wide * (kt // tq), qi, lambda j, c: tile(j, c, tq, None), state)
    half = tq // 2
    for hh in range(2):
        width = (hh + 1) * half
        row = hh * half + lax.broadcasted_iota(jnp.int32, (half, width), 0)
        col = lax.broadcasted_iota(jnp.int32, (half, width), 1)
        tile(qi, state, width, _mask_bias(col <= row), rows=slice(hh * half, (hh + 1) * half), align=tq)


def _sparse_proj_body(x_ref, an_ref, sc_ref, sh_ref, w_ref, cos_ref, sin_ref, hg_ref,
                      mq_ref, mka_ref, mkb_ref, mva_ref, mvb_ref, km_ref, nq_ref, ck_ref, cv_ref,
                      ksa_ref, ksb_ref, vsa_ref, vsb_ref, kw_ref, vwa_ref, vwb_ref, gt_ref,
                      *, per_b):
    rows = x_ref.shape[0]
    h = _norm_mod(x_ref[...], an_ref[...], sc_ref[0], sh_ref[0]).astype(BF16)
    cos = cos_ref[...]
    sin = sin_ref[...]
    lane = _lane_iota(rows)
    lo = lane < HEAD_DIM
    seq_pos = (pl.program_id(0) % per_b) * rows + lax.broadcasted_iota(jnp.int32, (rows, LANES), 0)
    lane_blk = lane & (HEAD_DIM - 1)
    moba_hot = jnp.where(lane_blk == seq_pos // MOBA_BLOCK, 1.0, 0.0)
    sel_hot = jnp.where(lane_blk == seq_pos // NSA_SEL_BLOCK, 1.0, 0.0)

    def proj(c0, n):
        return jnp.dot(h, w_ref[:, c0:c0 + n], preferred_element_type=F32)

    def group(y, p):
        return y[:, p * LANES:(p + 1) * LANES]

    def nr(y, ss, p, gi):
        return _head_norm_rope(group(y, p), group(ss, p), hg_ref[gi:gi + 1, :], cos, sin)

    def split_groups(y, fill, a_ref, b_ref):
        rolled = pltpu.roll(y, HEAD_DIM, 1)
        a_ref[0, 0] = jnp.where(lo, y, fill).astype(a_ref.dtype)
        b_ref[0, 0] = jnp.where(lo, fill, rolled).astype(b_ref.dtype)
        a_ref[0, 1] = jnp.where(lo, rolled, fill).astype(a_ref.dtype)
        b_ref[0, 1] = jnp.where(lo, fill, y).astype(b_ref.dtype)

    moba_w = MOBA_HEADS * HEAD_DIM
    pairs = moba_w // LANES
    y = proj(0, moba_w)
    ss = _head_sumsq(y)
    for p in range(pairs):
        mq_ref[:, p * LANES:(p + 1) * LANES] = nr(y, ss, p, 0)
    y = proj(moba_w, moba_w)
    ss = _head_sumsq(y)
    for p in range(pairs):
        sl = slice(p * LANES, (p + 1) * LANES)
        k = nr(y, ss, p, 1)
        mka_ref[:, sl] = jnp.where(lo, k, moba_hot).astype(mka_ref.dtype)
        mkb_ref[:, sl] = jnp.where(lo, moba_hot, k).astype(mkb_ref.dtype)
        for r in range(rows // MOBA_BLOCK):
            km_ref[r, :, sl] = jnp.mean(k[r * MOBA_BLOCK:(r + 1) * MOBA_BLOCK], axis=0, keepdims=True)
    y = proj(2 * moba_w, moba_w)
    for p in range(pairs):
        sl = slice(p * LANES, (p + 1) * LANES)
        mva_ref[:, sl] = jnp.where(lo, group(y, p), 1.0).astype(mva_ref.dtype)
        mvb_ref[:, sl] = jnp.where(lo, 1.0, group(y, p)).astype(mvb_ref.dtype)
    c0 = 3 * moba_w
    y = proj(c0, NSA_HEADS * HEAD_DIM)
    ss = _head_sumsq(y)
    for p in range(NSA_HEADS * HEAD_DIM // LANES):
        nq_ref[:, p * LANES:(p + 1) * LANES] = nr(y, ss, p, 2)
    c0 += NSA_HEADS * HEAD_DIM
    y = proj(c0, 6 * LANES)
    ss = _head_sumsq(y)
    kc = nr(y, ss, 0, 3)
    ck_ref[0, 0] = kc[:, :HEAD_DIM]
    ck_ref[0, 1] = kc[:, HEAD_DIM:]
    vc = group(y, 1)
    cv_ref[0, 0] = vc[:, :HEAD_DIM]
    cv_ref[0, 1] = vc[:, HEAD_DIM:]
    split_groups(nr(y, ss, 2, 4), sel_hot, ksa_ref, ksb_ref)
    split_groups(group(y, 3), 1.0, vsa_ref, vsb_ref)
    kw = nr(y, ss, 4, 5)
    kw_rolled = pltpu.roll(kw, HEAD_DIM, 1)
    kw_ref[0, 0] = jnp.where(lo, kw, kw_rolled).astype(kw_ref.dtype)
    kw_ref[0, 1] = jnp.where(lo, kw_rolled, kw).astype(kw_ref.dtype)
    split_groups(group(y, 5), 1.0, vwa_ref, vwb_ref)
    c0 += 6 * LANES
    gt_ref[...] = _sigmoid(proj(c0, LANES))


def _sparse_proj(x2, an, sc, sh, w_pad, cos, sin, hgains, batch, seq):
    t, d = x2.shape
    rows = min(PROJ_ROWS, seq)
    per_b = seq // rows
    nb = seq // MOBA_BLOCK
    g = NSA_KV_HEADS
    w512 = MOBA_HEADS * HEAD_DIM
    row_spec = lambda w: pl.BlockSpec((rows, w), lambda i: (i, 0))
    mod_spec = pl.BlockSpec((1, 1, d), lambda i: (i // per_b, 0, 0))
    grp64 = pl.BlockSpec((1, g, rows, HEAD_DIM), lambda i: (i // per_b, 0, i % per_b, 0))
    grp128 = pl.BlockSpec((1, g, rows, LANES), lambda i: (i // per_b, 0, i % per_b, 0))
    tok = lambda dt: jax.ShapeDtypeStruct((t, w512), dt)
    kv = lambda dt, w: jax.ShapeDtypeStruct((batch, g, seq, w), dt)
    out_shape = [tok(F32), tok(BF16), tok(BF16), tok(BF16), tok(BF16),
                 jax.ShapeDtypeStruct((batch * nb, 1, w512), F32),
                 tok(F32),
                 kv(F32, HEAD_DIM), kv(F32, HEAD_DIM),
                 kv(BF16, LANES), kv(BF16, LANES), kv(BF16, LANES), kv(BF16, LANES),
                 kv(BF16, LANES), kv(BF16, LANES), kv(BF16, LANES),
                 jax.ShapeDtypeStruct((t, LANES), F32)]
    out_specs = ([row_spec(w512)] * 5
                 + [pl.BlockSpec((rows // MOBA_BLOCK, 1, w512), lambda i: (i, 0, 0)), row_spec(w512),
                    grp64, grp64] + [grp128] * 7 + [row_spec(LANES)])
    return pl.pallas_call(
        functools.partial(_sparse_proj_body, per_b=per_b),
        out_shape=out_shape,
        grid=(t // rows,),
        in_specs=[row_spec(d), _const_spec((1, d)), mod_spec, mod_spec,
                  _const_spec(w_pad.shape), row_spec(LANES), row_spec(LANES),
                  _const_spec(hgains.shape)],
        out_specs=out_specs,
        compiler_params=_cparams(1),
        name="sparse_in_proj",
    )(x2, an, sc, sh, w_pad, cos, sin, hgains)


def _moba_body(q_ref, ka_ref, kb_ref, va_ref, vb_ref, km_ref, o_ref, acc_ref):
    qi = pl.program_id(2)
    tq = q_ref.shape[1]
    kt = min(K_TILE, ka_ref.shape[1])
    nb = km_ref.shape[1]
    pairs = q_ref.shape[2] // LANES
    lo = _lane_iota(tq) < HEAD_DIM
    blk = lax.broadcasted_iota(jnp.int32, (nb, tq), 0)
    own = (qi * tq + lax.broadcasted_iota(jnp.int32, (nb, tq), 1)) // MOBA_BLOCK
    past = blk < own
    qp = []
    for pr in range(pairs):
        sl = slice(pr * LANES, (pr + 1) * LANES)
        q = q_ref[0, :, sl]
        km = km_ref[0, :, sl]
        for half in (lo, ~lo):
            gate = lax.dot_general(km, jnp.where(half, q, 0.0), NT, precision=HI, preferred_element_type=F32)
            top = _topk_rows(jnp.where(past, gate, NEG_INF), min(MOBA_TOPK, nb)) & past
            bias = _lane_spread(jnp.where(top | (blk == own), 1.0, 0.0))
            qp.append(jnp.where(half, q * QK_SCALE, bias).astype(BF16))
    acc_ref[...] = jnp.zeros_like(acc_ref)
    k_refs = (ka_ref, kb_ref)
    v_refs = (va_ref, vb_ref)

    def tile(j, ms, width, mask, rows=slice(None), align=None):
        keys = _key_rows(j, width, align)
        ks = [r[0, keys, :] for r in k_refs]
        vs = [r[0, keys, :] for r in v_refs]
        lanes = lambda a, c: a[:, (c // 2) * LANES:(c // 2 + 1) * LANES]
        s = [lax.dot_general(qp[c][rows], lanes(ks[c % 2], c), NT, preferred_element_type=F32)
             for c in range(2 * pairs)]
        out = []
        for c in range(2 * pairs):
            m, alpha, p = _softmax_step(ms[c][rows], s[c], mask)
            _accumulate(acc_ref.at[c, rows], alpha, p, lanes(vs[c % 2], c))
            out.append(m)
        return tuple(out)

    m0 = jnp.full((tq, 1), M_INIT, F32)
    _causal_sweep(qi, tq, kt, tile, (m0,) * (2 * pairs))
    for pr in range(pairs):
        o = jnp.where(lo, _normalized(acc_ref[2 * pr]), _normalized(acc_ref[2 * pr + 1]))
        o_ref[0, :, pr * LANES:(pr + 1) * LANES] = o.astype(o_ref.dtype)


def _moba_attention(mq, mka, mkb, mva, mvb, km):
    batch, seq, w = mq.shape
    tq = min(Q_TILE, seq)
    nb = seq // MOBA_BLOCK
    gw = MOBA_PAIRS_PER_STEP * LANES
    q_spec = pl.BlockSpec((1, tq, gw), lambda b, p, i: (b, i, p))
    kv_spec = pl.BlockSpec((1, seq, gw), lambda b, p, i: (b, 0, p))
    return pl.pallas_call(
        _moba_body,
        out_shape=jax.ShapeDtypeStruct((batch, seq, w), BF16),
        grid=(batch, w // gw, seq // tq),
        in_specs=[q_spec, kv_spec, kv_spec, kv_spec, kv_spec,
                  pl.BlockSpec((1, nb, gw), lambda b, p, i: (b, 0, p))],
        out_specs=q_spec,
        scratch_shapes=[pltpu.VMEM((2 * MOBA_PAIRS_PER_STEP, tq, LANES), F32)],
        compiler_params=_cparams(3),
        name="moba_attention",
    )(mq, mka, mkb, mva, mvb, km)


def _compress_body(t_ref, pos_ref, w1a_ref, w1b_ref, w2_ref, o_ref, *, transposed):
    t = t_ref[0]
    r = t.shape[0]
    a = _dot_3pass(t + pos_ref[0:1, :], w1a_ref[...])
    b = _dot_3pass(t + pos_ref[1:2, :], w1b_ref[...])
    hid = _silu(a + pltpu.roll(b, r - 1, 0))
    out = _dot_3pass(hid, w2_ref[...])
    rowi = lax.broadcasted_iota(jnp.int32, out.shape, 0)
    out = jnp.where(rowi < r - 1, out, 0.0)
    o_ref[0] = out.T if transposed else out


def _compress(t, pos, w1, w2, transposed):
    n, r, w = t.shape
    out_dims = (LANES, r) if transposed else (r, LANES)
    half = NSA_CMP_LEN // 2
    hidden = w1.shape[-1]
    w1a = w1[:half].reshape(w, hidden)
    w1b = w1[half:].reshape(w, hidden)
    w2d = jnp.concatenate([w2, w2], axis=1)
    return pl.pallas_call(
        functools.partial(_compress_body, transposed=transposed),
        out_shape=jax.ShapeDtypeStruct((n,) + out_dims, F32),
        grid=(n,),
        in_specs=[pl.BlockSpec((1, r, w), lambda i: (i, 0, 0)), _const_spec((2, w)),
                  _const_spec((w, hidden)), _const_spec((w, hidden)), _const_spec((hidden, LANES))],
        out_specs=pl.BlockSpec((1,) + out_dims, lambda i: (i, 0, 0)),
        compiler_params=_cparams(1),
        name="nsa_compress",
    )(t, pos.reshape(2, w), w1a, w1b, w2d)


def _nsa_cmp_body(q_ref, kc_ref, vct_ref, ovt_ref, oc_ref, bias_ref):
    qi = pl.program_id(2)
    tq = q_ref.shape[1]
    r = kc_ref.shape[1]
    nsel = ovt_ref.shape[0]
    kc_terms = _split_bf16(kc_ref[0], 2)
    vct_b = vct_ref[0].astype(BF16)
    lo = _lane_iota(tq) < HEAD_DIM
    n = lax.broadcasted_iota(jnp.int32, (r, tq), 0)
    pos = qi * tq + lax.broadcasted_iota(jnp.int32, (r, tq), 1)
    ok = (n * NSA_CMP_STRIDE + (NSA_CMP_LEN - 1) <= pos) & (n < r - 1)
    psum = jnp.zeros((r, tq), F32)
    outs = []
    for hd in range(NSA_GROUP):
        qp = q_ref[0, :, (hd // 2) * LANES:(hd // 2 + 1) * LANES]
        qm = jnp.where(lo if hd % 2 == 0 else ~lo, qp, 0.0)
        s = _dot_nt_3pass(kc_terms, qm) * SCALE
        s = jnp.where(ok, s, NEG_INF)
        e = jnp.where(ok, jnp.exp(s - jnp.max(s, axis=0, keepdims=True)), 0.0)
        l = jnp.sum(e, axis=0, keepdims=True)
        p = e * jnp.where(l > 0.0, 1.0 / l, 0.0)
        psum = psum + p
        outs.append(jnp.dot(vct_b, p.astype(BF16), preferred_element_type=F32))
    even_rows = lax.broadcasted_iota(jnp.int32, (LANES, tq), 0) < HEAD_DIM
    for pr in range(NSA_GROUP // 2):
        oc_ref[0, :, pr * LANES:(pr + 1) * LANES] = jnp.where(even_rows, outs[2 * pr], outs[2 * pr + 1]).T
    ovt = ovt_ref[...].astype(BF16)
    imp = sum(jnp.dot(ovt, t, preferred_element_type=F32) for t in _split_bf16(psum, 3))
    blk = lax.broadcasted_iota(jnp.int32, (nsel, tq), 0)
    cur = (qi * tq + lax.broadcasted_iota(jnp.int32, (nsel, tq), 1)) // NSA_SEL_BLOCK
    ok_blk = blk <= cur
    forced = ok_blk & ((blk == 0) | (blk >= cur - 1))
    score = jnp.where(ok_blk, jnp.where(forced, NSA_FORCE, imp), NEG_INF)
    chosen = _topk_rows(score, min(NSA_SEL_TOPK, nsel), tq // NSA_SEL_BLOCK, qi + 1) & ok_blk
    bias_ref[0, 0] = _lane_spread(jnp.where(chosen, 1.0, 0.0))


def _nsa_overlap_t(r, nsel):
    cs = np.arange(r) * NSA_CMP_STRIDE
    ss = np.arange(nsel) * NSA_SEL_BLOCK
    ov = (np.minimum(cs[None, :] + NSA_CMP_LEN, ss[:, None] + NSA_SEL_BLOCK)
          - np.maximum(cs[None, :], ss[:, None]))
    return jnp.asarray(np.clip(ov, 0, None) / NSA_CMP_LEN, dtype=F32)


def _nsa_cmp_attention(nq, kc, vct, batch, seq):
    g = NSA_KV_HEADS
    tq = min(CMP_TILE, seq)
    r = kc.shape[1]
    nsel = seq // NSA_SEL_BLOCK
    gw = NSA_GROUP * HEAD_DIM
    return pl.pallas_call(
        _nsa_cmp_body,
        out_shape=[jax.ShapeDtypeStruct((batch, seq, g * gw), F32),
                   jax.ShapeDtypeStruct((batch, g, seq, LANES), F32)],
        grid=(batch, g, seq // tq),
        in_specs=[pl.BlockSpec((1, tq, gw), lambda b, gi, i: (b, i, gi)),
                  pl.BlockSpec((1, r, LANES), lambda b, gi, i: (b * g + gi, 0, 0)),
                  pl.BlockSpec((1, LANES, r), lambda b, gi, i: (b * g + gi, 0, 0)),
                  _const_spec((nsel, r))],
        out_specs=[pl.BlockSpec((1, tq, gw), lambda b, gi, i: (b, i, gi)),
                   pl.BlockSpec((1, 1, tq, LANES), lambda b, gi, i: (b, gi, i, 0))],
        compiler_params=_cparams(3),
        name="nsa_compressed_attention",
    )(nq, kc, vct, _nsa_overlap_t(r, nsel))


def _nsa_main_body(q_ref, ksa_ref, ksb_ref, vsa_ref, vsb_ref, kw_ref, vwa_ref, vwb_ref,
                   bias_ref, gt_ref, oc_ref, o_ref, acc_ref):
    gi = pl.program_id(1)
    qi = pl.program_id(2)
    tq = q_ref.shape[1]
    kt = min(K_TILE, ksa_ref.shape[2])
    lane = _lane_iota(tq)
    lo = lane < HEAD_DIM
    bias = bias_ref[0, 0]
    q_sel, q_win = [], []
    for hd in range(NSA_GROUP):
        qs = q_ref[0, :, (hd // 2) * LANES:(hd // 2 + 1) * LANES] * QK_SCALE
        half = lo if hd % 2 == 0 else ~lo
        q_sel.append(jnp.where(half, qs, bias).astype(BF16))
        q_win.append(jnp.where(half, qs, 0.0).astype(BF16))
    k_sel = [r.at[0] for r in (ksa_ref, ksb_ref)]
    v_sel = [r.at[0] for r in (vsa_ref, vsb_ref)]
    v_win = [r.at[0] for r in (vwa_ref, vwb_ref)]

    def tile(j, ms, width, mask, rows=slice(None), align=None, *, qs, ks, vs):
        keys = _key_rows(j, width, align)
        s = [lax.dot_general(qs[hd][rows], ks[hd % 2][0, keys, :], NT, preferred_element_type=F32)
             for hd in range(NSA_GROUP)]
        out = []
        for hd in range(NSA_GROUP):
            m, alpha, p = _softmax_step(ms[hd][rows], s[hd], mask)
            _accumulate(acc_ref.at[hd, rows], alpha, p, vs[hd % 2][0, keys, :])
            out.append(m)
        return tuple(out)

    m0 = (jnp.full((tq, 1), M_INIT, F32),) * NSA_GROUP

    acc_ref[...] = jnp.zeros_like(acc_ref)
    _causal_sweep(qi, tq, kt, functools.partial(tile, qs=q_sel, ks=k_sel, vs=v_sel), m0)
    o_sel = [_normalized(acc_ref[hd]) for hd in range(NSA_GROUP)]

    acc_ref[...] = jnp.zeros_like(acc_ref)
    k_win = [kw_ref.at[0]] * 2
    half = tq // 2
    span = NSA_WINDOW + half
    row_minus_col = (lax.broadcasted_iota(jnp.int32, (half, span), 0)
                     - lax.broadcasted_iota(jnp.int32, (half, span), 1))
    for hh in range(2):
        start = qi * tq + hh * half
        first = jnp.maximum(start - NSA_WINDOW, 0)
        delta = start - first + row_minus_col
        tile(first // half, m0, span, _mask_bias((delta >= 0) & (delta < NSA_WINDOW)),
             rows=slice(hh * half, (hh + 1) * half), align=half, qs=q_win, ks=k_win, vs=v_win)
    o_win = [_normalized(acc_ref[hd]) for hd in range(NSA_GROUP)]

    gt = gt_ref[0]

    def gate(branch, hd):
        c = branch * NSA_HEADS + gi * NSA_GROUP + hd
        return jnp.sum(jnp.where(lane == c, gt, 0.0), axis=-1, keepdims=True)

    for pr in range(NSA_GROUP // 2):
        a, b = 2 * pr, 2 * pr + 1
        o = (jnp.where(lo, gate(0, a), gate(0, b)) * oc_ref[0, :, pr * LANES:(pr + 1) * LANES]
             + jnp.where(lo, gate(1, a) * o_sel[a], gate(1, b) * o_sel[b])
             + jnp.where(lo, gate(2, a) * o_win[a], gate(2, b) * o_win[b]))
        o_ref[0, :, pr * LANES:(pr + 1) * LANES] = o.astype(o_ref.dtype)


def _nsa_main_attention(nq, ksa, ksb, vsa, vsb, kw, vwa, vwb, bias, gates, oc):
    batch, seq, w = nq.shape
    g = NSA_KV_HEADS
    tq = min(Q_TILE, seq)
    gw = NSA_GROUP * HEAD_DIM
    kv_spec = pl.BlockSpec((1, 1, seq, LANES), lambda b, gi, i: (b, gi, 0, 0))
    q_spec = pl.BlockSpec((1, tq, gw), lambda b, gi, i: (b, i, gi))
    return pl.pallas_call(
        _nsa_main_body,
        out_shape=jax.ShapeDtypeStruct((batch, seq, w), BF16),
        grid=(batch, g, seq // tq),
        in_specs=[q_spec] + [kv_spec] * 7
                 + [pl.BlockSpec((1, 1, tq, LANES), lambda b, gi, i: (b, gi, i, 0)),
                    pl.BlockSpec((1, tq, LANES), lambda b, gi, i: (b, i, 0)),
                    q_spec],
        out_specs=q_spec,
        scratch_shapes=[pltpu.VMEM((NSA_GROUP, tq, LANES), F32)],
        compiler_params=_cparams(3),
        name="nsa_selected_window_attention",
    )(nq, ksa, ksb, vsa, vsb, kw, vwa, vwb, bias, gates, oc)


def _diff_proj_body(x_ref, an_ref, sc_ref, sh_ref, w_ref, cos_ref, sin_ref, hg_ref,
                    q_ref, k_ref, v_ref):
    h = _norm_mod(x_ref[...], an_ref[...], sc_ref[0], sh_ref[0]).astype(BF16)
    cos = cos_ref[...]
    sin = sin_ref[...]
    w = q_ref.shape[1]
    chunk = 4 * LANES
    for c0 in range(0, w, chunk):
        yq = jnp.dot(h, w_ref[:, c0:c0 + chunk], preferred_element_type=F32)
        yk = jnp.dot(h, w_ref[:, w + c0:w + c0 + chunk], preferred_element_type=F32)
        yv = jnp.dot(h, w_ref[:, 2 * w + c0:2 * w + c0 + chunk], preferred_element_type=F32)
        sq, sk = _head_sumsq(yq), _head_sumsq(yk)
        for p in range(chunk // LANES):
            sl = slice(p * LANES, (p + 1) * LANES)
            dst = slice(c0 + p * LANES, c0 + (p + 1) * LANES)
            q_ref[:, dst] = (_head_norm_rope(yq[:, sl], sq[:, sl], hg_ref[0:1, :], cos, sin) * QK_SCALE).astype(q_ref.dtype)
            k_ref[:, dst] = _head_norm_rope(yk[:, sl], sk[:, sl], hg_ref[1:2, :], cos, sin).astype(k_ref.dtype)
        v_ref[:, c0:c0 + chunk] = yv.astype(v_ref.dtype)


def _diff_proj(x2, an, sc, sh, w, cos, sin, hgains, seq):
    t, d = x2.shape
    rows = min(PROJ_ROWS, seq)
    per_b = seq // rows
    wq = w.shape[1] // 3
    row_spec = lambda n: pl.BlockSpec((rows, n), lambda i: (i, 0))
    mod_spec = pl.BlockSpec((1, 1, d), lambda i: (i // per_b, 0, 0))
    return pl.pallas_call(
        _diff_proj_body,
        out_shape=[jax.ShapeDtypeStruct((t, wq), BF16)] * 3,
        grid=(t // rows,),
        in_specs=[row_spec(d), _const_spec((1, d)), mod_spec, mod_spec, _const_spec(w.shape),
                  row_spec(LANES), row_spec(LANES), _const_spec(hgains.shape)],
        out_specs=[row_spec(wq)] * 3,
        compiler_params=_cparams(1),
        name="diff_in_proj",
    )(x2, an, sc, sh, w, cos, sin, hgains)


def _diff_body(lam_ref, on_ref, q_ref, k_ref, v_ref, o_ref, acc_ref, den_ref, *, lam_init):
    qi = pl.program_id(2)
    tq = q_ref.shape[1]
    kt = min(K_TILE, k_ref.shape[1])
    heads = q_ref.shape[2] // LANES
    lo = _lane_iota(tq) < HEAD_DIM
    qb = []
    for hd in range(heads):
        q = q_ref[0, :, hd * LANES:(hd + 1) * LANES]
        zero = jnp.zeros_like(q)
        qb += [jnp.where(lo, q, zero), jnp.where(lo, zero, q)]
    acc_ref[...] = jnp.zeros_like(acc_ref)
    den_ref[...] = jnp.zeros_like(den_ref)

    def tile(j, ms, width, mask, rows=slice(None), align=None):
        keys = _key_rows(j, width, align)
        k = k_ref[0, keys, :]
        v = v_ref[0, keys, :]
        s = [lax.dot_general(qb[c][rows], k[:, (c // 2) * LANES:(c // 2 + 1) * LANES], NT,
                             preferred_element_type=F32) for c in range(2 * heads)]
        out = []
        for c in range(2 * heads):
            m, alpha, p = _softmax_step(ms[c][rows], s[c], mask)
            parts = [p[:, g * LANES:(g + 1) * LANES] for g in range(width // LANES)]
            while len(parts) > 1:
                parts = [a + b for a, b in zip(parts[0::2], parts[1::2])]
            den_ref[c, rows] = alpha * den_ref[c, rows] + parts[0].astype(F32)
            _accumulate(acc_ref.at[c, rows], alpha, p, v[:, (c // 2) * LANES:(c // 2 + 1) * LANES])
            out.append(m)
        return tuple(out)

    m0 = jnp.full((tq, 1), M_INIT, F32)
    _causal_sweep(qi, tq, kt, tile, (m0,) * (2 * heads))
    lp = lam_ref[...]
    lam = (jnp.exp(jnp.sum(lp[0:1] * lp[1:2], axis=-1, keepdims=True))
           - jnp.exp(jnp.sum(lp[2:3] * lp[3:4], axis=-1, keepdims=True)) + lam_init)
    for hd in range(heads):
        den = [jnp.sum(den_ref[2 * hd + c], axis=-1, keepdims=True) for c in range(2)]
        o = acc_ref[2 * hd] / den[0] - lam * (acc_ref[2 * hd + 1] / den[1])
        ms = jnp.mean(o * o, axis=-1, keepdims=True)
        o = o * lax.rsqrt(ms + NORM_EPS) * on_ref[...] * (1.0 - lam_init)
        o_ref[0, :, hd * LANES:(hd + 1) * LANES] = o.astype(o_ref.dtype)


def _diff_attention(q, k, v, lam_params, out_norm, lam_init):
    batch, seq, w = q.shape
    tq = min(Q_TILE, seq)
    gw = DIFF_HEADS_PER_STEP * LANES
    kv_spec = pl.BlockSpec((1, seq, gw), lambda b, h, i: (b, 0, h))
    q_spec = pl.BlockSpec((1, tq, gw), lambda b, h, i: (b, i, h))
    return pl.pallas_call(
        functools.partial(_diff_body, lam_init=lam_init),
        out_shape=jax.ShapeDtypeStruct((batch, seq, w), BF16),
        grid=(batch, w // gw, seq // tq),
        in_specs=[_const_spec(lam_params.shape), _const_spec((1, LANES)), q_spec, kv_spec, kv_spec],
        out_specs=q_spec,
        scratch_shapes=[pltpu.VMEM((2 * DIFF_HEADS_PER_STEP, tq, LANES), F32)] * 2,
        compiler_params=_cparams(3),
        name="diff_attention",
    )(lam_params, out_norm.reshape(1, LANES), q, k, v)


def _out_ffn_body(*refs, n_attn):
    x_ref = refs[0]
    attn = refs[1:1 + 2 * n_attn]
    ga_ref, fn_ref, sc_ref, sh_ref, gf_ref, wg_ref, wu_ref, wd_ref, o_ref = refs[1 + 2 * n_attn:]
    y = None
    for a in range(n_attn):
        part = jnp.dot(attn[2 * a][...], attn[2 * a + 1][...], preferred_element_type=F32)
        y = part if y is None else y + part
    x1 = x_ref[...] + ga_ref[0] * y
    h = _norm_mod(x1, fn_ref[...], sc_ref[0], sh_ref[0]).astype(BF16)
    acc = jnp.zeros(x1.shape, F32)
    for c0 in range(0, wg_ref.shape[1], FFN_CHUNK):
        gch = jnp.dot(h, wg_ref[:, c0:c0 + FFN_CHUNK], preferred_element_type=F32)
        uch = jnp.dot(h, wu_ref[:, c0:c0 + FFN_CHUNK], preferred_element_type=F32)
        act = (_silu(gch) * uch).astype(BF16)
        acc = acc + jnp.dot(act, wd_ref[c0:c0 + FFN_CHUNK, :], preferred_element_type=F32)
    o_ref[...] = x1 + gf_ref[0] * acc


def _out_ffn(x2, attn_pairs, ga, fn, sc, sh, gf, wg, wu, wd, seq):
    t, d = x2.shape
    rows = min(PROJ_ROWS, seq)
    per_b = seq // rows
    row_spec = lambda n: pl.BlockSpec((rows, n), lambda i: (i, 0))
    mod_spec = pl.BlockSpec((1, 1, d), lambda i: (i // per_b, 0, 0))
    resident = lambda a: pl.BlockSpec(a.shape, lambda i: (0, 0), pipeline_mode=pl.Buffered(1))
    args, specs = [x2], [row_spec(d)]
    for o, w in attn_pairs:
        args += [o, w]
        specs += [row_spec(o.shape[1]), resident(w)]
    args += [ga, fn, sc, sh, gf, wg, wu, wd]
    specs += [mod_spec, _const_spec((1, d)), mod_spec, mod_spec, mod_spec,
              resident(wg), resident(wu), resident(wd)]
    return pl.pallas_call(
        functools.partial(_out_ffn_body, n_attn=len(attn_pairs)),
        out_shape=jax.ShapeDtypeStruct((t, d), F32),
        grid=(t // rows,),
        in_specs=specs,
        out_specs=row_spec(d),
        compiler_params=_cparams(1),
        name="out_proj_swiglu",
    )(*args)


def _rope_lane_tables(positions):
    inv_freq = 1.0 / (ROPE_THETA ** (jnp.arange(0, ROT_DIM, 2, dtype=F32) / ROT_DIM))
    lane = np.arange(LANES) % HEAD_DIM
    rotary = lane < ROT_DIM
    freq = jnp.where(rotary, inv_freq[lane % ROT_HALF], 0.0)
    sign = jnp.asarray(np.where(rotary, np.where(lane < ROT_HALF, -1.0, 1.0), 0.0), F32)
    ang = positions.astype(F32).reshape(-1, 1) * freq
    return jnp.where(rotary, jnp.cos(ang), 1.0), jnp.sin(ang) * sign


def _pair_gain(gain):
    return jnp.concatenate([gain, gain]).reshape(1, LANES)


def _pad_rows(a, rows=8):
    return jnp.concatenate([a, jnp.zeros((rows - a.shape[0], a.shape[1]), a.dtype)], axis=0)


def kernel(x, c, positions, ada_w, ada_b, attn_norm, ffn_norm, ffn_w_gate, ffn_w_up, ffn_w_down, sp_w_in, sp_w_out, moba_q_norm, moba_k_norm, nsa_q_norm, nsa_k_norm, nsa_cmp_pos, nsa_cmp_w1, nsa_cmp_w2, diff_w_in, diff_w_out, diff_q_norm, diff_k_norm, diff_lambda, diff_out_norm):
    batch, seq, d = x.shape
    depth = ada_w.shape[0]
    assert seq % K_TILE == 0 and K_TILE % Q_TILE == 0 and Q_TILE == TAIL_TILE
    assert seq % MOBA_BLOCK == 0 and NSA_WINDOW % Q_TILE == 0 and NSA_WINDOW + Q_TILE <= seq
    assert (seq // MOBA_BLOCK) % SUBLANES == 0 and (seq // NSA_SEL_BLOCK) % SUBLANES == 0
    assert seq // MOBA_BLOCK <= HEAD_DIM and seq // NSA_SEL_BLOCK <= HEAD_DIM
    g = NSA_KV_HEADS
    cos, sin = _rope_lane_tables(positions)
    mod = _modulation(c, ada_w, ada_b).reshape(depth, batch, 6, 1, d)
    x2 = x.reshape(batch * seq, d)
    for i in range(depth):
        sh_a, sc_a, g_a, sh_f, sc_f, g_f = (mod[i, :, n] for n in range(6))
        an = attn_norm[i].reshape(1, d)
        j = i // 2
        to3 = lambda a: a.reshape(batch, seq, a.shape[-1])
        if i % 2 == 0:
            w_in = sp_w_in[j]
            pad = (-w_in.shape[1]) % LANES
            w_pad = jnp.concatenate([w_in, jnp.zeros((d, pad), w_in.dtype)], axis=1).astype(BF16)
            hgains = _pad_rows(jnp.concatenate(
                [_pair_gain(moba_q_norm[j]), _pair_gain(moba_k_norm[j]), _pair_gain(nsa_q_norm[j])]
                + [_pair_gain(nsa_k_norm[j, n]) for n in range(3)], axis=0))
            (mq, mka, mkb, mva, mvb, km, nq, ck, cv, ksa, ksb, vsa, vsb, kw, vwa, vwb, gates) = _sparse_proj(
                x2, an, sc_a, sh_a, w_pad, cos, sin, hgains, batch, seq)
            w512 = MOBA_HEADS * HEAD_DIM
            o_moba = _moba_attention(to3(mq), to3(mka), to3(mkb), to3(mva), to3(mvb),
                                     km.reshape(batch, seq // MOBA_BLOCK, w512))
            r = seq // NSA_CMP_STRIDE
            cw = NSA_CMP_STRIDE * HEAD_DIM
            kc = _compress(ck.reshape(batch * g, r, cw), nsa_cmp_pos[j, 0], nsa_cmp_w1[j, 0], nsa_cmp_w2[j, 0], False)
            vct = _compress(cv.reshape(batch * g, r, cw), nsa_cmp_pos[j, 1], nsa_cmp_w1[j, 1], nsa_cmp_w2[j, 1], True)
            oc, bias = _nsa_cmp_attention(to3(nq), kc, vct, batch, seq)
            o_nsa = _nsa_main_attention(to3(nq), ksa, ksb, vsa, vsb, kw, vwa, vwb, bias, to3(gates), oc)
            w_out = sp_w_out[j].astype(BF16)
            attn_pairs = [(o_moba.reshape(batch * seq, w512), w_out[:w512]),
                          (o_nsa.reshape(batch * seq, w512), w_out[w512:])]
        else:
            lam_init = 0.8 - 0.6 * float(np.exp(-0.3 * i))
            hgains = _pad_rows(jnp.concatenate(
                [_pair_gain(diff_q_norm[j]), _pair_gain(diff_k_norm[j])], axis=0))
            q, k, v = _diff_proj(x2, an, sc_a, sh_a, diff_w_in[j].astype(BF16), cos, sin, hgains, seq)
            o = _diff_attention(to3(q), to3(k), to3(v), diff_lambda[j], diff_out_norm[j], lam_init)
            attn_pairs = [(o.reshape(batch * seq, o.shape[-1]), diff_w_out[j].astype(BF16))]
        x2 = _out_ffn(x2, attn_pairs, g_a, ffn_norm[i].reshape(1, d), sc_f, sh_f, g_f,
                      ffn_w_gate[i].astype(BF16), ffn_w_up[i].astype(BF16),
                      ffn_w_down[i].astype(BF16), seq)
    return x2.reshape(batch, seq, d)
```

```python
import functools

import numpy as np
import jax
import jax.numpy as jnp
from jax import lax
from jax.experimental import pallas as pl
from jax.experimental.pallas import tpu as pltpu

F32 = jnp.float32
BF16 = jnp.bfloat16
HI = lax.Precision.HIGHEST

LANES = 128
HEAD_DIM = 64
ROT_DIM = HEAD_DIM // 4
ROT_HALF = ROT_DIM // 2
ROPE_THETA = 500000.0
NORM_EPS = 1e-6
NEG_INF = -1e30
M_INIT = -1e20
SCALE = HEAD_DIM ** -0.5
QK_SCALE = SCALE * 1.4426950408889634

MOBA_HEADS = 8
MOBA_BLOCK = 256
MOBA_TOPK = 3

NSA_HEADS = 8
NSA_KV_HEADS = 2
NSA_GROUP = NSA_HEADS // NSA_KV_HEADS
NSA_CMP_LEN = 32
NSA_CMP_STRIDE = 16
NSA_SEL_BLOCK = 64
NSA_SEL_TOPK = 16
NSA_WINDOW = 512
NSA_FORCE = 1e4

DIFF_HEADS = 8
DIFF_HEADS_PER_STEP = 2
MOBA_PAIRS_PER_STEP = 4

Q_TILE = 512
K_TILE = 1024
TAIL_TILE = 512
CMP_TILE = 1024
SUBLANES = 8
PROJ_ROWS = 1024
FFN_CHUNK = 256
VMEM_LIMIT = 56 * 2 ** 20

NT = (((1,), (1,)), ((), ()))


def _cparams(n_axes):
    return pltpu.CompilerParams(dimension_semantics=("arbitrary",) * n_axes,
                                vmem_limit_bytes=VMEM_LIMIT)


def _const_spec(shape):
    return pl.BlockSpec(shape, lambda *_: (0,) * len(shape))


def _silu(t):
    return t / (1.0 + jnp.exp(-t))


def _sigmoid(t):
    return 1.0 / (1.0 + jnp.exp(-t))


def _lane_iota(rows):
    return lax.broadcasted_iota(jnp.int32, (rows, LANES), 1)


def _mod_body(c_ref, w_ref, b_ref, o_ref):
    a = _silu(c_ref[...])
    o_ref[0] = jnp.dot(a, w_ref[0], precision=HI, preferred_element_type=F32) + b_ref[0]


def _modulation(c, ada_w, ada_b):
    depth, d, n = ada_w.shape
    b = c.shape[0]
    tn = n // 4
    return pl.pallas_call(
        _mod_body,
        out_shape=jax.ShapeDtypeStruct((depth, b, n), F32),
        grid=(depth, n // tn),
        in_specs=[pl.BlockSpec((b, d), lambda i, j: (0, 0)),
                  pl.BlockSpec((1, d, tn), lambda i, j: (i, 0, j)),
                  pl.BlockSpec((1, 1, tn), lambda i, j: (i, 0, j))],
        out_specs=pl.BlockSpec((1, b, tn), lambda i, j: (i, 0, j)),
        compiler_params=_cparams(2),
        name="adaln_modulation",
    )(c, ada_w, ada_b.reshape(depth, 1, n))


def _norm_mod(x, gain, sc, sh):
    ms = jnp.mean(x * x, axis=-1, keepdims=True)
    return x * lax.rsqrt(ms + NORM_EPS) * gain * (1.0 + sc) + sh


def _head_sumsq(y):
    wide = 2 * LANES
    same_head = (lax.broadcasted_iota(jnp.int32, (wide, wide), 0) // HEAD_DIM
                 == lax.broadcasted_iota(jnp.int32, (wide, wide), 1) // HEAD_DIM)
    ones_bd = jnp.where(same_head, 1.0, 0.0).astype(BF16)
    y2 = (y * y).astype(BF16)
    out = [jnp.dot(y2[:, c:c + wide], ones_bd, preferred_element_type=F32)
           for c in range(0, y.shape[1], wide)]
    return out[0] if len(out) == 1 else jnp.concatenate(out, axis=1)


def _head_norm_rope(y, ss, gain, cos, sin):
    rows = y.shape[0]
    lane = _lane_iota(rows)
    yn = y * lax.rsqrt(ss * (1.0 / HEAD_DIM) + NORM_EPS) * gain
    ahead = pltpu.roll(yn, LANES - ROT_HALF, 1)
    behind = pltpu.roll(yn, ROT_HALF, 1)
    partner = jnp.where((lane & (HEAD_DIM - 1)) < ROT_HALF, ahead, behind)
    return yn * cos + partner * sin


def _split_bf16(a, terms):
    out = []
    for _ in range(terms):
        t = a.astype(BF16)
        out.append(t)
        a = a - t.astype(F32)
    return out


def _dot_nt_3pass(a_terms, b):
    a_hi, a_lo = a_terms
    b_hi, b_lo = _split_bf16(b, 2)
    d = lambda u, v: lax.dot_general(u, v, NT, preferred_element_type=F32)
    return d(a_hi, b_hi) + (d(a_hi, b_lo) + d(a_lo, b_hi))


def _dot_3pass(a, b):
    a_hi, a_lo = _split_bf16(a, 2)
    b_hi, b_lo = _split_bf16(b, 2)
    d = lambda u, v: jnp.dot(u, v, preferred_element_type=F32)
    return d(a_hi, b_hi) + (d(a_hi, b_lo) + d(a_lo, b_hi))


def _softmax_step(m, s, bias):
    sb = s.astype(BF16)
    if bias is not None:
        sb = sb + bias
    m_new = jnp.maximum(m, jnp.max(sb, axis=-1, keepdims=True).astype(F32))
    return m_new, jnp.exp2(m - m_new), jnp.exp2(sb - m_new.astype(BF16))


def _mask_bias(mask):
    return jnp.where(mask, 0.0, NEG_INF).astype(BF16)


def _accumulate(acc_ref, alpha, p, v):
    acc_ref[...] = alpha * acc_ref[...] + jnp.dot(p, v, preferred_element_type=F32)


def _normalized(acc):
    return acc / pltpu.roll(acc, HEAD_DIM, 1)


def _topk_rows(score, k, group=None, live_groups=None):
    n, cols = score.shape
    nblk = n // SUBLANES
    blocks = [score[b * SUBLANES:(b + 1) * SUBLANES] for b in range(nblk)]
    sub = lax.broadcasted_iota(jnp.int32, (SUBLANES, cols), 0)

    def count(ranks, lo_row, hi_row):
        ranks = list(ranks)
        for i in range(lo_row, hi_row):
            si = score[i:i + 1, :]
            for b in range(nblk):
                if b > i // SUBLANES:
                    beats = si >= blocks[b]
                elif b < i // SUBLANES:
                    beats = si > blocks[b]
                else:
                    beats = (si > blocks[b]) | ((si == blocks[b]) & (sub > i % SUBLANES))
                ranks[b] = ranks[b] + jnp.where(beats, 1, 0)
        return tuple(ranks)

    ranks = tuple(jnp.zeros((SUBLANES, cols), jnp.int32) for _ in range(nblk))
    if group is None:
        ranks = count(ranks, 0, n)
    else:
        for g in range(n // group):
            ranks = lax.cond(g < live_groups, functools.partial(count, lo_row=g * group, hi_row=(g + 1) * group),
                             lambda r: r, ranks)
    return jnp.concatenate(ranks, axis=0) < k


def _lane_spread(allowed):
    n, cols = allowed.shape
    parts = [allowed, allowed]
    if n < HEAD_DIM:
        fill = jnp.zeros((HEAD_DIM - n, cols), F32)
        parts = [allowed, fill, allowed, fill]
    return jnp.where(jnp.concatenate(parts, axis=0).T > 0.5, 0.0, NEG_INF)


def _key_rows(j, width, align):
    align = align or width
    return pl.ds(pl.multiple_of(j * align, align), width)


def _causal_sweep(qi, tq, kt, tile, state):
    n_wide = (qi * tq) // kt
    state = lax.fori_loop(0, n_wide, lambda j, c: tile(j, c, kt, None), state)
    state = lax.fori_loop(n_wide * (kt // tq), qi, lambda j, c: tile(j, c, tq, None), state)
    half = tq // 2
    for hh in range(2):
        width = (hh + 1) * half
        row = hh * half + lax.broadcasted_iota(jnp.int32, (half, width), 0)
        col = lax.broadcasted_iota(jnp.int32, (half, width), 1)
        tile(qi, state, width, _mask_bias(col <= row), rows=slice(hh * half, (hh + 1) * half), align=tq)


def _sparse_proj_body(x_ref, an_ref, sc_ref, sh_ref, w_ref, cos_ref, sin_ref, hg_ref,
                      mq_ref, mka_ref, mkb_ref, mva_ref, mvb_ref, km_ref, nq_ref, ck_ref, cv_ref,
                      ksa_ref, ksb_ref, vsa_ref, vsb_ref, kw_ref, vwa_ref, vwb_ref, gt_ref,
                      *, per_b):
    rows = x_ref.shape[0]
    h = _norm_mod(x_ref[...], an_ref[...], sc_ref[0], sh_ref[0]).astype(BF16)
    cos = cos_ref[...]
    sin = sin_ref[...]
    lane = _lane_iota(rows)
    lo = lane < HEAD_DIM
    seq_pos = (pl.program_id(0) % per_b) * rows + lax.broadcasted_iota(jnp.int32, (rows, LANES), 0)
    lane_blk = lane & (HEAD_DIM - 1)
    moba_hot = jnp.where(lane_blk == seq_pos // MOBA_BLOCK, 1.0, 0.0)
    sel_hot = jnp.where(lane_blk == seq_pos // NSA_SEL_BLOCK, 1.0, 0.0)

    def proj(c0, n):
        return jnp.dot(h, w_ref[:, c0:c0 + n], preferred_element_type=F32)

    def group(y, p):
        return y[:, p * LANES:(p + 1) * LANES]

    def nr(y, ss, p, gi):
        return _head_norm_rope(group(y, p), group(ss, p), hg_ref[gi:gi + 1, :], cos, sin)

    def split_groups(y, fill, a_ref, b_ref):
        rolled = pltpu.roll(y, HEAD_DIM, 1)
        a_ref[0, 0] = jnp.where(lo, y, fill).astype(a_ref.dtype)
        b_ref[0, 0] = jnp.where(lo, fill, rolled).astype(b_ref.dtype)
        a_ref[0, 1] = jnp.where(lo, rolled, fill).astype(a_ref.dtype)
        b_ref[0, 1] = jnp.where(lo, fill, y).astype(b_ref.dtype)

    moba_w = MOBA_HEADS * HEAD_DIM
    pairs = moba_w // LANES
    y = proj(0, moba_w)
    ss = _head_sumsq(y)
    for p in range(pairs):
        mq_ref[:, p * LANES:(p + 1) * LANES] = nr(y, ss, p, 0)
    y = proj(moba_w, moba_w)
    ss = _head_sumsq(y)
    for p in range(pairs):
        sl = slice(p * LANES, (p + 1) * LANES)
        k = nr(y, ss, p, 1)
        mka_ref[:, sl] = jnp.where(lo, k, moba_hot).astype(mka_ref.dtype)
        mkb_ref[:, sl] = jnp.where(lo, moba_hot, k).astype(mkb_ref.dtype)
        for r in range(rows // MOBA_BLOCK):
            km_ref[r, :, sl] = jnp.mean(k[r * MOBA_BLOCK:(r + 1) * MOBA_BLOCK], axis=0, keepdims=True)
    y = proj(2 * moba_w, moba_w)
    for p in range(pairs):
        sl = slice(p * LANES, (p + 1) * LANES)
        mva_ref[:, sl] = jnp.where(lo, group(y, p), 1.0).astype(mva_ref.dtype)
        mvb_ref[:, sl] = jnp.where(lo, 1.0, group(y, p)).astype(mvb_ref.dtype)
    c0 = 3 * moba_w
    y = proj(c0, NSA_HEADS * HEAD_DIM)
    ss = _head_sumsq(y)
    for p in range(NSA_HEADS * HEAD_DIM // LANES):
        nq_ref[:, p * LANES:(p + 1) * LANES] = nr(y, ss, p, 2)
    c0 += NSA_HEADS * HEAD_DIM
    y = proj(c0, 6 * LANES)
    ss = _head_sumsq(y)
    kc = nr(y, ss, 0, 3)
    ck_ref[0, 0] = kc[:, :HEAD_DIM]
    ck_ref[0, 1] = kc[:, HEAD_DIM:]
    vc = group(y, 1)
    cv_ref[0, 0] = vc[:, :HEAD_DIM]
    cv_ref[0, 1] = vc[:, HEAD_DIM:]
    split_groups(nr(y, ss, 2, 4), sel_hot, ksa_ref, ksb_ref)
    split_groups(group(y, 3), 1.0, vsa_ref, vsb_ref)
    kw = nr(y, ss, 4, 5)
    kw_rolled = pltpu.roll(kw, HEAD_DIM, 1)
    kw_ref[0, 0] = jnp.where(lo, kw, kw_rolled).astype(kw_ref.dtype)
    kw_ref[0, 1] = jnp.where(lo, kw_rolled, kw).astype(kw_ref.dtype)
    split_groups(group(y, 5), 1.0, vwa_ref, vwb_ref)
    c0 += 6 * LANES
    gt_ref[...] = _sigmoid(proj(c0, LANES))


def _sparse_proj(x2, an, sc, sh, w_pad, cos, sin, hgains, batch, seq):
    t, d = x2.shape
    rows = min(PROJ_ROWS, seq)
    per_b = seq // rows
    nb = seq // MOBA_BLOCK
    g = NSA_KV_HEADS
    w512 = MOBA_HEADS * HEAD_DIM
    row_spec = lambda w: pl.BlockSpec((rows, w), lambda i: (i, 0))
    mod_spec = pl.BlockSpec((1, 1, d), lambda i: (i // per_b, 0, 0))
    grp64 = pl.BlockSpec((1, g, rows, HEAD_DIM), lambda i: (i // per_b, 0, i % per_b, 0))
    grp128 = pl.BlockSpec((1, g, rows, LANES), lambda i: (i // per_b, 0, i % per_b, 0))
    tok = lambda dt: jax.ShapeDtypeStruct((t, w512), dt)
    kv = lambda dt, w: jax.ShapeDtypeStruct((batch, g, seq, w), dt)
    out_shape = [tok(F32), tok(BF16), tok(BF16), tok(BF16), tok(BF16),
                 jax.ShapeDtypeStruct((batch * nb, 1, w512), F32),
                 tok(F32),
                 kv(F32, HEAD_DIM), kv(F32, HEAD_DIM),
                 kv(BF16, LANES), kv(BF16, LANES), kv(BF16, LANES), kv(BF16, LANES),
                 kv(BF16, LANES), kv(BF16, LANES), kv(BF16, LANES),
                 jax.ShapeDtypeStruct((t, LANES), F32)]
    out_specs = ([row_spec(w512)] * 5
                 + [pl.BlockSpec((rows // MOBA_BLOCK, 1, w512), lambda i: (i, 0, 0)), row_spec(w512),
                    grp64, grp64] + [grp128] * 7 + [row_spec(LANES)])
    return pl.pallas_call(
        functools.partial(_sparse_proj_body, per_b=per_b),
        out_shape=out_shape,
        grid=(t // rows,),
        in_specs=[row_spec(d), _const_spec((1, d)), mod_spec, mod_spec,
                  _const_spec(w_pad.shape), row_spec(LANES), row_spec(LANES),
                  _const_spec(hgains.shape)],
        out_specs=out_specs,
        compiler_params=_cparams(1),
        name="sparse_in_proj",
    )(x2, an, sc, sh, w_pad, cos, sin, hgains)


def _moba_body(q_ref, ka_ref, kb_ref, va_ref, vb_ref, km_ref, o_ref, acc_ref):
    qi = pl.program_id(2)
    tq = q_ref.shape[1]
    kt = min(K_TILE, ka_ref.shape[1])
    nb = km_ref.shape[1]
    pairs = q_ref.shape[2] // LANES
    lo = _lane_iota(tq) < HEAD_DIM
    blk = lax.broadcasted_iota(jnp.int32, (nb, tq), 0)
    own = (qi * tq + lax.broadcasted_iota(jnp.int32, (nb, tq), 1)) // MOBA_BLOCK
    past = blk < own
    qp = []
    for pr in range(pairs):
        sl = slice(pr * LANES, (pr + 1) * LANES)
        q = q_ref[0, :, sl]
        km = km_ref[0, :, sl]
        for half in (lo, ~lo):
            gate = lax.dot_general(km, jnp.where(half, q, 0.0), NT, precision=HI, preferred_element_type=F32)
            top = _topk_rows(jnp.where(past, gate, NEG_INF), min(MOBA_TOPK, nb)) & past
            bias = _lane_spread(jnp.where(top | (blk == own), 1.0, 0.0))
            qp.append(jnp.where(half, q * QK_SCALE, bias).astype(BF16))
    acc_ref[...] = jnp.zeros_like(acc_ref)
    k_refs = (ka_ref, kb_ref)
    v_refs = (va_ref, vb_ref)

    def tile(j, ms, width, mask, rows=slice(None), align=None):
        keys = _key_rows(j, width, align)
        ks = [r[0, keys, :] for r in k_refs]
        vs = [r[0, keys, :] for r in v_refs]
        lanes = lambda a, c: a[:, (c // 2) * LANES:(c // 2 + 1) * LANES]
        s = [lax.dot_general(qp[c][rows], lanes(ks[c % 2], c), NT, preferred_element_type=F32)
             for c in range(2 * pairs)]
        out = []
        for c in range(2 * pairs):
            m, alpha, p = _softmax_step(ms[c][rows], s[c], mask)
            _accumulate(acc_ref.at[c, rows], alpha, p, lanes(vs[c % 2], c))
            out.append(m)
        return tuple(out)

    m0 = jnp.full((tq, 1), M_INIT, F32)
    _causal_sweep(qi, tq, kt, tile, (m0,) * (2 * pairs))
    for pr in range(pairs):
        o = jnp.where(lo, _normalized(acc_ref[2 * pr]), _normalized(acc_ref[2 * pr + 1]))
        o_ref[0, :, pr * LANES:(pr + 1) * LANES] = o.astype(o_ref.dtype)


def _moba_attention(mq, mka, mkb, mva, mvb, km):
    batch, seq, w = mq.shape
    tq = min(Q_TILE, seq)
    nb = seq // MOBA_BLOCK
    gw = MOBA_PAIRS_PER_STEP * LANES
    q_spec = pl.BlockSpec((1, tq, gw), lambda b, p, i: (b, i, p))
    kv_spec = pl.BlockSpec((1, seq, gw), lambda b, p, i: (b, 0, p))
    return pl.pallas_call(
        _moba_body,
        out_shape=jax.ShapeDtypeStruct((batch, seq, w), BF16),
        grid=(batch, w // gw, seq // tq),
        in_specs=[q_spec, kv_spec, kv_spec, kv_spec, kv_spec,
                  pl.BlockSpec((1, nb, gw), lambda b, p, i: (b, 0, p))],
        out_specs=q_spec,
        scratch_shapes=[pltpu.VMEM((2 * MOBA_PAIRS_PER_STEP, tq, LANES), F32)],
        compiler_params=_cparams(3),
        name="moba_attention",
    )(mq, mka, mkb, mva, mvb, km)


def _compress_body(t_ref, pos_ref, w1a_ref, w1b_ref, w2_ref, o_ref, *, transposed):
    t = t_ref[0]
    r = t.shape[0]
    a = _dot_3pass(t + pos_ref[0:1, :], w1a_ref[...])
    b = _dot_3pass(t + pos_ref[1:2, :], w1b_ref[...])
    hid = _silu(a + pltpu.roll(b, r - 1, 0))
    out = _dot_3pass(hid, w2_ref[...])
    rowi = lax.broadcasted_iota(jnp.int32, out.shape, 0)
    out = jnp.where(rowi < r - 1, out, 0.0)
    o_ref[0] = out.T if transposed else out


def _compress(t, pos, w1, w2, transposed):
    n, r, w = t.shape
    out_dims = (LANES, r) if transposed else (r, LANES)
    half = NSA_CMP_LEN // 2
    hidden = w1.shape[-1]
    w1a = w1[:half].reshape(w, hidden)
    w1b = w1[half:].reshape(w, hidden)
    w2d = jnp.concatenate([w2, w2], axis=1)
    return pl.pallas_call(
        functools.partial(_compress_body, transposed=transposed),
        out_shape=jax.ShapeDtypeStruct((n,) + out_dims, F32),
        grid=(n,),
        in_specs=[pl.BlockSpec((1, r, w), lambda i: (i, 0, 0)), _const_spec((2, w)),
                  _const_spec((w, hidden)), _const_spec((w, hidden)), _const_spec((hidden, LANES))],
        out_specs=pl.BlockSpec((1,) + out_dims, lambda i: (i, 0, 0)),
        compiler_params=_cparams(1),
        name="nsa_compress",
    )(t, pos.reshape(2, w), w1a, w1b, w2d)


def _nsa_cmp_body(q_ref, kc_ref, vct_ref, ovt_ref, oc_ref, bias_ref):
    qi = pl.program_id(2)
    tq = q_ref.shape[1]
    r = kc_ref.shape[1]
    nsel = ovt_ref.shape[0]
    kc_terms = _split_bf16(kc_ref[0], 2)
    vct_b = vct_ref[0].astype(BF16)
    lo = _lane_iota(tq) < HEAD_DIM
    n = lax.broadcasted_iota(jnp.int32, (r, tq), 0)
    pos = qi * tq + lax.broadcasted_iota(jnp.int32, (r, tq), 1)
    ok = (n * NSA_CMP_STRIDE + (NSA_CMP_LEN - 1) <= pos) & (n < r - 1)
    psum = jnp.zeros((r, tq), F32)
    outs = []
    for hd in range(NSA_GROUP):
        qp = q_ref[0, :, (hd // 2) * LANES:(hd // 2 + 1) * LANES]
        qm = jnp.where(lo if hd % 2 == 0 else ~lo, qp, 0.0)
        s = _dot_nt_3pass(kc_terms, qm) * SCALE
        s = jnp.where(ok, s, NEG_INF)
        e = jnp.where(ok, jnp.exp(s - jnp.max(s, axis=0, keepdims=True)), 0.0)
        l = jnp.sum(e, axis=0, keepdims=True)
        p = e * jnp.where(l > 0.0, 1.0 / l, 0.0)
        psum = psum + p
        outs.append(jnp.dot(vct_b, p.astype(BF16), preferred_element_type=F32))
    even_rows = lax.broadcasted_iota(jnp.int32, (LANES, tq), 0) < HEAD_DIM
    for pr in range(NSA_GROUP // 2):
        oc_ref[0, :, pr * LANES:(pr + 1) * LANES] = jnp.where(even_rows, outs[2 * pr], outs[2 * pr + 1]).T
    ovt = ovt_ref[...].astype(BF16)
    imp = sum(jnp.dot(ovt, t, preferred_element_type=F32) for t in _split_bf16(psum, 3))
    blk = lax.broadcasted_iota(jnp.int32, (nsel, tq), 0)
    cur = (qi * tq + lax.broadcasted_iota(jnp.int32, (nsel, tq), 1)) // NSA_SEL_BLOCK
    ok_blk = blk <= cur
    forced = ok_blk & ((blk == 0) | (blk >= cur - 1))
    score = jnp.where(ok_blk, jnp.where(forced, NSA_FORCE, imp), NEG_INF)
    chosen = _topk_rows(score, min(NSA_SEL_TOPK, nsel), tq // NSA_SEL_BLOCK, qi + 1) & ok_blk
    bias_ref[0, 0] = _lane_spread(jnp.where(chosen, 1.0, 0.0))


def _nsa_overlap_t(r, nsel):
    cs = np.arange(r) * NSA_CMP_STRIDE
    ss = np.arange(nsel) * NSA_SEL_BLOCK
    ov = (np.minimum(cs[None, :] + NSA_CMP_LEN, ss[:, None] + NSA_SEL_BLOCK)
          - np.maximum(cs[None, :], ss[:, None]))
    return jnp.asarray(np.clip(ov, 0, None) / NSA_CMP_LEN, dtype=F32)


def _nsa_cmp_attention(nq, kc, vct, batch, seq):
    g = NSA_KV_HEADS
    tq = min(CMP_TILE, seq)
    r = kc.shape[1]
    nsel = seq // NSA_SEL_BLOCK
    gw = NSA_GROUP * HEAD_DIM
    return pl.pallas_call(
        _nsa_cmp_body,
        out_shape=[jax.ShapeDtypeStruct((batch, seq, g * gw), F32),
                   jax.ShapeDtypeStruct((batch, g, seq, LANES), F32)],
        grid=(batch, g, seq // tq),
        in_specs=[pl.BlockSpec((1, tq, gw), lambda b, gi, i: (b, i, gi)),
                  pl.BlockSpec((1, r, LANES), lambda b, gi, i: (b * g + gi, 0, 0)),
                  pl.BlockSpec((1, LANES, r), lambda b, gi, i: (b * g + gi, 0, 0)),
                  _const_spec((nsel, r))],
        out_specs=[pl.BlockSpec((1, tq, gw), lambda b, gi, i: (b, i, gi)),
                   pl.BlockSpec((1, 1, tq, LANES), lambda b, gi, i: (b, gi, i, 0))],
        compiler_params=_cparams(3),
        name="nsa_compressed_attention",
    )(nq, kc, vct, _nsa_overlap_t(r, nsel))


def _nsa_main_body(q_ref, ksa_ref, ksb_ref, vsa_ref, vsb_ref, kw_ref, vwa_ref, vwb_ref,
                   bias_ref, gt_ref, oc_ref, o_ref, acc_ref):
    gi = pl.program_id(1)
    qi = pl.program_id(2)
    tq = q_ref.shape[1]
    kt = min(K_TILE, ksa_ref.shape[2])
    lane = _lane_iota(tq)
    lo = lane < HEAD_DIM
    bias = bias_ref[0, 0]
    q_sel, q_win = [], []
    for hd in range(NSA_GROUP):
        qs = q_ref[0, :, (hd // 2) * LANES:(hd // 2 + 1) * LANES] * QK_SCALE
        half = lo if hd % 2 == 0 else ~lo
        q_sel.append(jnp.where(half, qs, bias).astype(BF16))
        q_win.append(jnp.where(half, qs, 0.0).astype(BF16))
    k_sel = [r.at[0] for r in (ksa_ref, ksb_ref)]
    v_sel = [r.at[0] for r in (vsa_ref, vsb_ref)]
    v_win = [r.at[0] for r in (vwa_ref, vwb_ref)]

    def tile(j, ms, width, mask, rows=slice(None), align=None, *, qs, ks, vs):
        keys = _key_rows(j, width, align)
        s = [lax.dot_general(qs[hd][rows], ks[hd % 2][0, keys, :], NT, preferred_element_type=F32)
             for hd in range(NSA_GROUP)]
        out = []
        for hd in range(NSA_GROUP):
            m, alpha, p = _softmax_step(ms[hd][rows], s[hd], mask)
            _accumulate(acc_ref.at[hd, rows], alpha, p, vs[hd % 2][0, keys, :])
            out.append(m)
        return tuple(out)

    m0 = (jnp.full((tq, 1), M_INIT, F32),) * NSA_GROUP

    acc_ref[...] = jnp.zeros_like(acc_ref)
    _causal_sweep(qi, tq, kt, functools.partial(tile, qs=q_sel, ks=k_sel, vs=v_sel), m0)
    o_sel = [_normalized(acc_ref[hd]) for hd in range(NSA_GROUP)]

    acc_ref[...] = jnp.zeros_like(acc_ref)
    k_win = [kw_ref.at[0]] * 2
    half = tq // 2
    span = NSA_WINDOW + half
    row_minus_col = (lax.broadcasted_iota(jnp.int32, (half, span), 0)
                     - lax.broadcasted_iota(jnp.int32, (half, span), 1))
    for hh in range(2):
        start = qi * tq + hh * half
        first = jnp.maximum(start - NSA_WINDOW, 0)
        delta = start - first + row_minus_col
        tile(first // half, m0, span, _mask_bias((delta >= 0) & (delta < NSA_WINDOW)),
             rows=slice(hh * half, (hh + 1) * half), align=half, qs=q_win, ks=k_win, vs=v_win)
    o_win = [_normalized(acc_ref[hd]) for hd in range(NSA_GROUP)]

    gt = gt_ref[0]

    def gate(branch, hd):
        c = branch * NSA_HEADS + gi * NSA_GROUP + hd
        return jnp.sum(jnp.where(lane == c, gt, 0.0), axis=-1, keepdims=True)

    for pr in range(NSA_GROUP // 2):
        a, b = 2 * pr, 2 * pr + 1
        o = (jnp.where(lo, gate(0, a), gate(0, b)) * oc_ref[0, :, pr * LANES:(pr + 1) * LANES]
             + jnp.where(lo, gate(1, a) * o_sel[a], gate(1, b) * o_sel[b])
             + jnp.where(lo, gate(2, a) * o_win[a], gate(2, b) * o_win[b]))
        o_ref[0, :, pr * LANES:(pr + 1) * LANES] = o.astype(o_ref.dtype)


def _nsa_main_attention(nq, ksa, ksb, vsa, vsb, kw, vwa, vwb, bias, gates, oc):
    batch, seq, w = nq.shape
    g = NSA_KV_HEADS
    tq = min(Q_TILE, seq)
    gw = NSA_GROUP * HEAD_DIM
    kv_spec = pl.BlockSpec((1, 1, seq, LANES), lambda b, gi, i: (b, gi, 0, 0))
    q_spec = pl.BlockSpec((1, tq, gw), lambda b, gi, i: (b, i, gi))
    return pl.pallas_call(
        _nsa_main_body,
        out_shape=jax.ShapeDtypeStruct((batch, seq, w), BF16),
        grid=(batch, g, seq // tq),
        in_specs=[q_spec] + [kv_spec] * 7
                 + [pl.BlockSpec((1, 1, tq, LANES), lambda b, gi, i: (b, gi, i, 0)),
                    pl.BlockSpec((1, tq, LANES), lambda b, gi, i: (b, i, 0)),
                    q_spec],
        out_specs=q_spec,
        scratch_shapes=[pltpu.VMEM((NSA_GROUP, tq, LANES), F32)],
        compiler_params=_cparams(3),
        name="nsa_selected_window_attention",
    )(nq, ksa, ksb, vsa, vsb, kw, vwa, vwb, bias, gates, oc)


def _diff_proj_body(x_ref, an_ref, sc_ref, sh_ref, w_ref, cos_ref, sin_ref, hg_ref,
                    q_ref, k_ref, v_ref):
    h = _norm_mod(x_ref[...], an_ref[...], sc_ref[0], sh_ref[0]).astype(BF16)
    cos = cos_ref[...]
    sin = sin_ref[...]
    w = q_ref.shape[1]
    chunk = 4 * LANES
    for c0 in range(0, w, chunk):
        yq = jnp.dot(h, w_ref[:, c0:c0 + chunk], preferred_element_type=F32)
        yk = jnp.dot(h, w_ref[:, w + c0:w + c0 + chunk], preferred_element_type=F32)
        yv = jnp.dot(h, w_ref[:, 2 * w + c0:2 * w + c0 + chunk], preferred_element_type=F32)
        sq, sk = _head_sumsq(yq), _head_sumsq(yk)
        for p in range(chunk // LANES):
            sl = slice(p * LANES, (p + 1) * LANES)
            dst = slice(c0 + p * LANES, c0 + (p + 1) * LANES)
            q_ref[:, dst] = (_head_norm_rope(yq[:, sl], sq[:, sl], hg_ref[0:1, :], cos, sin) * QK_SCALE).astype(q_ref.dtype)
            k_ref[:, dst] = _head_norm_rope(yk[:, sl], sk[:, sl], hg_ref[1:2, :], cos, sin).astype(k_ref.dtype)
        v_ref[:, c0:c0 + chunk] = yv.astype(v_ref.dtype)


def _diff_proj(x2, an, sc, sh, w, cos, sin, hgains, seq):
    t, d = x2.shape
    rows = min(PROJ_ROWS, seq)
    per_b = seq // rows
    wq = w.shape[1] // 3
    row_spec = lambda n: pl.BlockSpec((rows, n), lambda i: (i, 0))
    mod_spec = pl.BlockSpec((1, 1, d), lambda i: (i // per_b, 0, 0))
    return pl.pallas_call(
        _diff_proj_body,
        out_shape=[jax.ShapeDtypeStruct((t, wq), BF16)] * 3,
        grid=(t // rows,),
        in_specs=[row_spec(d), _const_spec((1, d)), mod_spec, mod_spec, _const_spec(w.shape),
                  row_spec(LANES), row_spec(LANES), _const_spec(hgains.shape)],
        out_specs=[row_spec(wq)] * 3,
        compiler_params=_cparams(1),
        name="diff_in_proj",
    )(x2, an, sc, sh, w, cos, sin, hgains)


def _diff_body(lam_ref, on_ref, q_ref, k_ref, v_ref, o_ref, acc_ref, den_ref, *, lam_init):
    qi = pl.program_id(2)
    tq = q_ref.shape[1]
    kt = min(K_TILE, k_ref.shape[1])
    heads = q_ref.shape[2] // LANES
    lo = _lane_iota(tq) < HEAD_DIM
    qb = []
    for hd in range(heads):
        q = q_ref[0, :, hd * LANES:(hd + 1) * LANES]
        zero = jnp.zeros_like(q)
        qb += [jnp.where(lo, q, zero), jnp.where(lo, zero, q)]
    acc_ref[...] = jnp.zeros_like(acc_ref)
    den_ref[...] = jnp.zeros_like(den_ref)

    def tile(j, ms, width, mask, rows=slice(None), align=None):
        keys = _key_rows(j, width, align)
        k = k_ref[0, keys, :]
        v = v_ref[0, keys, :]
        s = [lax.dot_general(qb[c][rows], k[:, (c // 2) * LANES:(c // 2 + 1) * LANES], NT,
                             preferred_element_type=F32) for c in range(2 * heads)]
        out = []
        for c in range(2 * heads):
            m, alpha, p = _softmax_step(ms[c][rows], s[c], mask)
            parts = [p[:, g * LANES:(g + 1) * LANES] for g in range(width // LANES)]
            while len(parts) > 1:
                parts = [a + b for a, b in zip(parts[0::2], parts[1::2])]
            den_ref[c, rows] = alpha * den_ref[c, rows] + parts[0].astype(F32)
            _accumulate(acc_ref.at[c, rows], alpha, p, v[:, (c // 2) * LANES:(c // 2 + 1) * LANES])
            out.append(m)
        return tuple(out)

    m0 = jnp.full((tq, 1), M_INIT, F32)
    _causal_sweep(qi, tq, kt, tile, (m0,) * (2 * heads))
    lp = lam_ref[...]
    lam = (jnp.exp(jnp.sum(lp[0:1] * lp[1:2], axis=-1, keepdims=True))
           - jnp.exp(jnp.sum(lp[2:3] * lp[3:4], axis=-1, keepdims=True)) + lam_init)
    for hd in range(heads):
        den = [jnp.sum(den_ref[2 * hd + c], axis=-1, keepdims=True) for c in range(2)]
        o = acc_ref[2 * hd] / den[0] - lam * (acc_ref[2 * hd + 1] / den[1])
        ms = jnp.mean(o * o, axis=-1, keepdims=True)
        o = o * lax.rsqrt(ms + NORM_EPS) * on_ref[...] * (1.0 - lam_init)
        o_ref[0, :, hd * LANES:(hd + 1) * LANES] = o.astype(o_ref.dtype)


def _diff_attention(q, k, v, lam_params, out_norm, lam_init):
    batch, seq, w = q.shape
    tq = min(Q_TILE, seq)
    gw = DIFF_HEADS_PER_STEP * LANES
    kv_spec = pl.BlockSpec((1, seq, gw), lambda b, h, i: (b, 0, h))
    q_spec = pl.BlockSpec((1, tq, gw), lambda b, h, i: (b, i, h))
    return pl.pallas_call(
        functools.partial(_diff_body, lam_init=lam_init),
        out_shape=jax.ShapeDtypeStruct((batch, seq, w), BF16),
        grid=(batch, w // gw, seq // tq),
        in_specs=[_const_spec(lam_params.shape), _const_spec((1, LANES)), q_spec, kv_spec, kv_spec],
        out_specs=q_spec,
        scratch_shapes=[pltpu.VMEM((2 * DIFF_HEADS_PER_STEP, tq, LANES), F32)] * 2,
        compiler_params=_cparams(3),
        name="diff_attention",
    )(lam_params, out_norm.reshape(1, LANES), q, k, v)


def _out_ffn_body(*refs, n_attn):
    x_ref = refs[0]
    attn = refs[1:1 + 2 * n_attn]
    ga_ref, fn_ref, sc_ref, sh_ref, gf_ref, wg_ref, wu_ref, wd_ref, o_ref = refs[1 + 2 * n_attn:]
    y = None
    for a in range(n_attn):
        part = jnp.dot(attn[2 * a][...], attn[2 * a + 1][...], preferred_element_type=F32)
        y = part if y is None else y + part
    x1 = x_ref[...] + ga_ref[0] * y
    h = _norm_mod(x1, fn_ref[...], sc_ref[0], sh_ref[0]).astype(BF16)
    acc = jnp.zeros(x1.shape, F32)
    for c0 in range(0, wg_ref.shape[1], FFN_CHUNK):
        gch = jnp.dot(h, wg_ref[:, c0:c0 + FFN_CHUNK], preferred_element_type=F32)
        uch = jnp.dot(h, wu_ref[:, c0:c0 + FFN_CHUNK], preferred_element_type=F32)
        act = (_silu(gch) * uch).astype(BF16)
        acc = acc + jnp.dot(act, wd_ref[c0:c0 + FFN_CHUNK, :], preferred_element_type=F32)
    o_ref[...] = x1 + gf_ref[0] * acc


def _out_ffn(x2, attn_pairs, ga, fn, sc, sh, gf, wg, wu, wd, seq):
    t, d = x2.shape
    rows = min(PROJ_ROWS, seq)
    per_b = seq // rows
    row_spec = lambda n: pl.BlockSpec((rows, n), lambda i: (i, 0))
    mod_spec = pl.BlockSpec((1, 1, d), lambda i: (i // per_b, 0, 0))
    resident = lambda a: pl.BlockSpec(a.shape, lambda i: (0, 0), pipeline_mode=pl.Buffered(1))
    args, specs = [x2], [row_spec(d)]
    for o, w in attn_pairs:
        args += [o, w]
        specs += [row_spec(o.shape[1]), resident(w)]
    args += [ga, fn, sc, sh, gf, wg, wu, wd]
    specs += [mod_spec, _const_spec((1, d)), mod_spec, mod_spec, mod_spec,
              resident(wg), resident(wu), resident(wd)]
    return pl.pallas_call(
        functools.partial(_out_ffn_body, n_attn=len(attn_pairs)),
        out_shape=jax.ShapeDtypeStruct((t, d), F32),
        grid=(t // rows,),
        in_specs=specs,
        out_specs=row_spec(d),
        compiler_params=_cparams(1),
        name="out_proj_swiglu",
    )(*args)


def _rope_lane_tables(positions):
    inv_freq = 1.0 / (ROPE_THETA ** (jnp.arange(0, ROT_DIM, 2, dtype=F32) / ROT_DIM))
    lane = np.arange(LANES) % HEAD_DIM
    rotary = lane < ROT_DIM
    freq = jnp.where(rotary, inv_freq[lane % ROT_HALF], 0.0)
    sign = jnp.asarray(np.where(rotary, np.where(lane < ROT_HALF, -1.0, 1.0), 0.0), F32)
    ang = positions.astype(F32).reshape(-1, 1) * freq
    return jnp.where(rotary, jnp.cos(ang), 1.0), jnp.sin(ang) * sign


def _pair_gain(gain):
    return jnp.concatenate([gain, gain]).reshape(1, LANES)


def _pad_rows(a, rows=8):
    return jnp.concatenate([a, jnp.zeros((rows - a.shape[0], a.shape[1]), a.dtype)], axis=0)


def kernel(x, c, positions, ada_w, ada_b, attn_norm, ffn_norm, ffn_w_gate, ffn_w_up, ffn_w_down, sp_w_in, sp_w_out, moba_q_norm, moba_k_norm, nsa_q_norm, nsa_k_norm, nsa_cmp_pos, nsa_cmp_w1, nsa_cmp_w2, diff_w_in, diff_w_out, diff_q_norm, diff_k_norm, diff_lambda, diff_out_norm):
    batch, seq, d = x.shape
    depth = ada_w.shape[0]
    assert seq % K_TILE == 0 and K_TILE % Q_TILE == 0 and Q_TILE == TAIL_TILE
    assert seq % MOBA_BLOCK == 0 and NSA_WINDOW % Q_TILE == 0 and NSA_WINDOW + Q_TILE <= seq
    assert (seq // MOBA_BLOCK) % SUBLANES == 0 and (seq // NSA_SEL_BLOCK) % SUBLANES == 0
    assert seq // MOBA_BLOCK <= HEAD_DIM and seq // NSA_SEL_BLOCK <= HEAD_DIM
    g = NSA_KV_HEADS
    cos, sin = _rope_lane_tables(positions)
    mod = _modulation(c, ada_w, ada_b).reshape(depth, batch, 6, 1, d)
    x2 = x.reshape(batch * seq, d)
    for i in range(depth):
        sh_a, sc_a, g_a, sh_f, sc_f, g_f = (mod[i, :, n] for n in range(6))
        an = attn_norm[i].reshape(1, d)
        j = i // 2
        to3 = lambda a: a.reshape(batch, seq, a.shape[-1])
        if i % 2 == 0:
            w_in = sp_w_in[j]
            pad = (-w_in.shape[1]) % LANES
            w_pad = jnp.concatenate([w_in, jnp.zeros((d, pad), w_in.dtype)], axis=1).astype(BF16)
            hgains = _pad_rows(jnp.concatenate(
                [_pair_gain(moba_q_norm[j]), _pair_gain(moba_k_norm[j]), _pair_gain(nsa_q_norm[j])]
                + [_pair_gain(nsa_k_norm[j, n]) for n in range(3)], axis=0))
            (mq, mka, mkb, mva, mvb, km, nq, ck, cv, ksa, ksb, vsa, vsb, kw, vwa, vwb, gates) = _sparse_proj(
                x2, an, sc_a, sh_a, w_pad, cos, sin, hgains, batch, seq)
            w512 = MOBA_HEADS * HEAD_DIM
            o_moba = _moba_attention(to3(mq), to3(mka), to3(mkb), to3(mva), to3(mvb),
                                     km.reshape(batch, seq // MOBA_BLOCK, w512))
            r = seq // NSA_CMP_STRIDE
            cw = NSA_CMP_STRIDE * HEAD_DIM
            kc = _compress(ck.reshape(batch * g, r, cw), nsa_cmp_pos[j, 0], nsa_cmp_w1[j, 0], nsa_cmp_w2[j, 0], False)
            vct = _compress(cv.reshape(batch * g, r, cw), nsa_cmp_pos[j, 1], nsa_cmp_w1[j, 1], nsa_cmp_w2[j, 1], True)
            oc, bias = _nsa_cmp_attention(to3(nq), kc, vct, batch, seq)
            o_nsa = _nsa_main_attention(to3(nq), ksa, ksb, vsa, vsb, kw, vwa, vwb, bias, to3(gates), oc)
            w_out = sp_w_out[j].astype(BF16)
            attn_pairs = [(o_moba.reshape(batch * seq, w512), w_out[:w512]),
                          (o_nsa.reshape(batch * seq, w512), w_out[w512:])]
        else:
            lam_init = 0.8 - 0.6 * float(np.exp(-0.3 * i))
            hgains = _pad_rows(jnp.concatenate(
                [_pair_gain(diff_q_norm[j]), _pair_gain(diff_k_norm[j])], axis=0))
            q, k, v = _diff_proj(x2, an, sc_a, sh_a, diff_w_in[j].astype(BF16), cos, sin, hgains, seq)
            o = _diff_attention(to3(q), to3(k), to3(v), diff_lambda[j], diff_out_norm[j], lam_init)
            attn_pairs = [(o.reshape(batch * seq, o.shape[-1]), diff_w_out[j].astype(BF16))]
        x2 = _out_ffn(x2, attn_pairs, g_a, ffn_norm[i].reshape(1, d), sc_f, sh_f, g_f,
                      ffn_w_gate[i].astype(BF16), ffn_w_up[i].astype(BF16),
                      ffn_w_down[i].astype(BF16), seq)
    return x2.reshape(batch, seq, d)
```
